```python
import jax, jax.numpy as jnp
from jax import lax
import numpy as np


D_MODEL = 1024
BATCH = 16
SEQ = 4096
DEPTH = 4

CONV_WIDTH = D_MODEL
CONV_TAPS = 31
MLSTM_WIDTH = D_MODEL
MLSTM_HEADS = 4
MLSTM_HEAD_DIM = MLSTM_WIDTH // MLSTM_HEADS
QK_CONV_TAPS = 4
MLSTM_CHUNK = 64
N_EXPERTS = 32
TOP_K = 4
D_FF = D_MODEL
SWIGLU_LIMIT = 7.0
SWIGLU_ALPHA = 1.702
EXPERT_BLOCK = 128
LN_EPS = 1e-5
DEEPNORM_ALPHA = (2 * DEPTH) ** 0.25
DEEPNORM_BETA = (8 * DEPTH) ** -0.25
OFF_GLU = 0
OFF_Q = OFF_GLU + 2 * CONV_WIDTH
OFF_K = OFF_Q + MLSTM_WIDTH
OFF_V = OFF_K + MLSTM_WIDTH
OFF_O = OFF_V + MLSTM_WIDTH
OFF_I = OFF_O + MLSTM_WIDTH
OFF_F = OFF_I + MLSTM_HEADS
OFF_GA = OFF_F + MLSTM_HEADS
OFF_GB = OFF_GA + D_MODEL
N_IN = OFF_GB + D_MODEL

kernel_name = 'hybrid_conformer_mlstm_moe_deepnorm'


def layer_norm(x, g, b):
    xf = x.astype(jnp.float32)
    xc = xf - xf.mean(-1, keepdims=True)
    var = jnp.mean(xc * xc, -1, keepdims=True)
    y = xc * lax.rsqrt(var + LN_EPS) * g.astype(jnp.float32) + b.astype(jnp.float32)
    return y.astype(x.dtype)


def head_layer_norm(h, g):
    B, S, H, dh = h.shape
    hf = h.astype(jnp.float32)
    hc = hf - hf.mean(-1, keepdims=True)
    var = jnp.mean(hc * hc, -1, keepdims=True)
    return (hc * lax.rsqrt(var + LN_EPS)).reshape(B, S, H * dh) * g.astype(jnp.float32)


def causal_depthwise_conv(u, w, b):
    taps, C = w.shape
    y = lax.conv_general_dilated(u, w[:, None, :].astype(u.dtype), window_strides=(1,),
                                 padding=[(taps - 1, 0)], dimension_numbers=('NWC', 'WIO', 'NWC'),
                                 feature_group_count=C)
    return y + b.astype(u.dtype)


def conformer_branch(u, dw_w, dw_b, ng, nb, out_w, out_b):
    a, g = jnp.split(u, 2, axis=-1)
    z = a * jax.nn.sigmoid(g)
    z = causal_depthwise_conv(z, dw_w, dw_b)
    z = jax.nn.silu(layer_norm(z, ng, nb))
    return z @ out_w + out_b


def mlstm_chunkwise(q, k, v, i_pre, f_pre):
    B, S, H, dh = q.shape
    L = MLSTM_CHUNK
    nc = S // L
    f32 = jnp.float32

    def to_chunks(t):
        return t.astype(f32).reshape(B, nc, L, H, -1).transpose(1, 0, 3, 2, 4)

    def gate_chunks(t):
        return t.astype(f32).reshape(B, nc, L, H).transpose(1, 0, 3, 2)

    qc = to_chunks(q)
    kc = to_chunks(k) * (dh ** -0.5)
    vc = to_chunks(v)
    ic = gate_chunks(i_pre)
    bc = jnp.cumsum(gate_chunks(jax.nn.log_sigmoid(f_pre.astype(f32))), axis=-1)
    mask = jnp.tril(jnp.ones((L, L), dtype=bool))

    def step(carry, inp):
        C, n, m = carry
        q_, k_, v_, b, ig = inp
        dmat = jnp.where(mask, b[..., :, None] - b[..., None, :] + ig[..., None, :], -jnp.inf)
        inter = b + m[..., None]
        m_row = jnp.maximum(inter, dmat.max(-1))
        w_intra = jnp.exp(dmat - m_row[..., None])
        w_inter = jnp.exp(inter - m_row)
        s = jnp.einsum('bhtk,bhsk->bhts', q_, k_) * w_intra
        num = w_inter[..., None] * jnp.einsum('bhvk,bhtk->bhtv', C, q_) + jnp.einsum('bhts,bhsv->bhtv', s, v_)
        den = w_inter * jnp.einsum('bhk,bhtk->bht', n, q_) + s.sum(-1)
        h = num / jnp.maximum(jnp.abs(den), jnp.exp(-m_row))[..., None]
        b_last = b[..., -1]
        log_ws = b_last[..., None] - b + ig
        m_new = jnp.maximum(b_last + m, log_ws.max(-1))
        ws = jnp.exp(log_ws - m_new[..., None])
        decay = jnp.exp(b_last + m - m_new)
        C_new = decay[..., None, None] * C + jnp.einsum('bhsv,bhsk->bhvk', v_ * ws[..., None], k_)
        n_new = decay[..., None] * n + jnp.einsum('bhs,bhsk->bhk', ws, k_)
        return (C_new, n_new, m_new), h

    init = (jnp.zeros((B, H, dh, dh), f32), jnp.zeros((B, H, dh), f32), jnp.zeros((B, H), f32))
    _, h = lax.scan(step, init, (qc, kc, vc, bc, ic))
    return h.transpose(1, 0, 3, 2, 4).reshape(B, S, H, dh)


def mlstm_branch(q, k, v, o, i_pre, f_pre, qk_conv_w, qk_conv_b, norm_g, out_w):
    B, S, _ = q.shape
    qk = jax.nn.silu(causal_depthwise_conv(jnp.concatenate([q, k], -1), qk_conv_w, qk_conv_b))
    q, k = jnp.split(qk, 2, axis=-1)
    shp = (B, S, MLSTM_HEADS, MLSTM_HEAD_DIM)
    h = mlstm_chunkwise(q.reshape(shp), k.reshape(shp), v.reshape(shp), i_pre, f_pre)
    h = head_layer_norm(h, norm_g).astype(o.dtype) * jax.nn.sigmoid(o)
    return h @ out_w


def moe_ffn(x2d, router_w, router_b, w1, b1, w2, b2):
    T, D = x2d.shape
    logits = (x2d @ router_w + router_b).astype(jnp.float32)
    top_logit, top_idx = lax.top_k(logits, TOP_K)
    gate = jax.nn.softmax(top_logit, axis=-1)
    n_assign = T * TOP_K
    flat_e = top_idx.reshape(-1).astype(jnp.int32)
    flat_tok = jnp.arange(n_assign, dtype=jnp.int32) // TOP_K
    order = jnp.argsort(flat_e, stable=True)
    se, stok, sw = flat_e[order], flat_tok[order], gate.reshape(-1)[order]
    counts = jnp.zeros((N_EXPERTS,), jnp.int32).at[flat_e].add(1)
    starts = jnp.cumsum(counts) - counts
    padded = (counts + EXPERT_BLOCK - 1) // EXPERT_BLOCK * EXPERT_BLOCK
    pends = jnp.cumsum(padded)
    pstarts = pends - padded
    dest = pstarts[se] + (jnp.arange(n_assign, dtype=jnp.int32) - starts[se])
    n_blocks = -(-n_assign // EXPERT_BLOCK) + N_EXPERTS
    n_rows = n_blocks * EXPERT_BLOCK
    row_tok = jnp.full((n_rows,), T, jnp.int32).at[dest].set(stok)
    row_w = jnp.zeros((n_rows,), jnp.float32).at[dest].set(sw)
    block_e = jnp.minimum(jnp.searchsorted(pends, jnp.arange(n_blocks, dtype=jnp.int32) * EXPERT_BLOCK,
                                           side='right'), N_EXPERTS - 1)
    x_pad = jnp.concatenate([x2d, jnp.zeros((1, D), x2d.dtype)], axis=0)

    def expert_block(acc, blk):
        tok, w, e = blk
        h = x_pad[tok] @ w1[e] + b1[e]
        g, lin = jnp.split(h, 2, axis=-1)
        g = jnp.minimum(g, SWIGLU_LIMIT)
        lin = jnp.clip(lin, -SWIGLU_LIMIT, SWIGLU_LIMIT)
        act = g * jax.nn.sigmoid(SWIGLU_ALPHA * g) * (lin + 1.0)
        y = act @ w2[e] + b2[e]
        return acc.at[tok].add((w[:, None] * y).astype(acc.dtype)), None

    acc, _ = lax.scan(expert_block, jnp.zeros((T + 1, D), x2d.dtype),
                      (row_tok.reshape(n_blocks, EXPERT_BLOCK), row_w.reshape(n_blocks, EXPERT_BLOCK), block_e))
    return acc[:T]


def hybrid_layer(x, w_in, b_in, conv_dw_w, conv_dw_b, conv_norm_g, conv_norm_b, conv_out_w, conv_out_b,
                 qk_conv_w, qk_conv_b, mlstm_norm_g, mlstm_out_w, w_out, ln1_g, ln1_b,
                 router_w, router_b, moe_w1, moe_b1, moe_w2, moe_b2, ln2_g, ln2_b):
    B, S, D = x.shape
    xp = x @ w_in + b_in
    y_conv = conformer_branch(xp[..., OFF_GLU:OFF_Q], conv_dw_w, conv_dw_b, conv_norm_g, conv_norm_b,
                              conv_out_w, conv_out_b)
    y_mem = mlstm_branch(xp[..., OFF_Q:OFF_K], xp[..., OFF_K:OFF_V], xp[..., OFF_V:OFF_O], xp[..., OFF_O:OFF_I],
                         xp[..., OFF_I:OFF_F], xp[..., OFF_F:OFF_GA], qk_conv_w, qk_conv_b, mlstm_norm_g, mlstm_out_w)
    mix = jax.nn.sigmoid(xp[..., OFF_GA:OFF_GB]) * y_conv + jax.nn.sigmoid(xp[..., OFF_GB:N_IN]) * y_mem
    x = layer_norm(DEEPNORM_ALPHA * x + mix @ w_out, ln1_g, ln1_b)
    y = moe_ffn(x.reshape(B * S, D), router_w, router_b, moe_w1, moe_b1, moe_w2, moe_b2).reshape(B, S, D)
    return layer_norm(DEEPNORM_ALPHA * x + y, ln2_g, ln2_b)


def setup_inputs(seed: int = 0) -> dict:
    key = jax.random.key(seed)
    ks = jax.random.split(key, 24)
    f32 = jnp.float32
    L, D, H = DEPTH, D_MODEL, MLSTM_HEADS

    def nrm(k, shape, scale):
        return jax.random.normal(k, shape, f32) * scale

    col_scale = jnp.concatenate([
        jnp.full((OFF_V,), D ** -0.5, f32),
        jnp.full((MLSTM_WIDTH,), DEEPNORM_BETA * D ** -0.5, f32),
        jnp.full((MLSTM_WIDTH,), D ** -0.5, f32),
        jnp.full((2 * H,), 0.1 * D ** -0.5, f32),
        jnp.full((2 * D,), D ** -0.5, f32)])
    w_in = nrm(ks[1], (L, D, N_IN), 1.0) * col_scale
    b_in = jnp.concatenate([
        nrm(ks[2], (L, OFF_I), 0.02),
        nrm(ks[3], (L, H), 0.1),
        jnp.linspace(3.0, 6.0, H, dtype=f32) + nrm(ks[4], (L, H), 0.1),
        nrm(ks[5], (L, 2 * D), 0.02)], axis=-1)
    return {
        'x': nrm(ks[0], (BATCH, SEQ, D), 1.0),
        'w_in': w_in,
        'b_in': b_in,
        'conv_dw_w': nrm(ks[6], (L, CONV_TAPS, CONV_WIDTH), CONV_TAPS ** -0.5),
        'conv_dw_b': nrm(ks[7], (L, CONV_WIDTH), 0.02),
        'conv_norm_g': 1.0 + nrm(ks[8], (L, CONV_WIDTH), 0.02),
        'conv_norm_b': nrm(ks[9], (L, CONV_WIDTH), 0.02),
        'conv_out_w': nrm(ks[10], (L, CONV_WIDTH, D), DEEPNORM_BETA * CONV_WIDTH ** -0.5),
        'conv_out_b': nrm(ks[11], (L, D), 0.02),
        'qk_conv_w': nrm(ks[12], (L, QK_CONV_TAPS, 2 * MLSTM_WIDTH), QK_CONV_TAPS ** -0.5),
        'qk_conv_b': nrm(ks[13], (L, 2 * MLSTM_WIDTH), 0.02),
        'mlstm_norm_g': 1.0 + nrm(ks[14], (L, MLSTM_WIDTH), 0.02),
        'mlstm_out_w': nrm(ks[15], (L, MLSTM_WIDTH, D), DEEPNORM_BETA * MLSTM_WIDTH ** -0.5),
        'w_out': nrm(ks[16], (L, D, D), DEEPNORM_BETA * D ** -0.5),
        'ln1_g': 1.0 + nrm(ks[17], (L, D), 0.02),
        'ln1_b': nrm(ks[18], (L, D), 0.02),
        'router_w': nrm(ks[19], (L, D, N_EXPERTS), D ** -0.5),
        'router_b': nrm(ks[20], (L, N_EXPERTS), 0.01),
        'moe_w1': nrm(ks[21], (L, N_EXPERTS, D, 2 * D_FF), D ** -0.5),
        'moe_b1': nrm(ks[22], (L, N_EXPERTS, 2 * D_FF), 0.02),
        'moe_w2': nrm(ks[23], (L, N_EXPERTS, D_FF, D), DEEPNORM_BETA * D_FF ** -0.5),
        'moe_b2': nrm(jax.random.fold_in(ks[23], 1), (L, N_EXPERTS, D), 0.02),
        'ln2_g': 1.0 + nrm(jax.random.fold_in(ks[17], 1), (L, D), 0.02),
        'ln2_b': nrm(jax.random.fold_in(ks[18], 1), (L, D), 0.02),
    }


def reference(x, w_in, b_in, conv_dw_w, conv_dw_b, conv_norm_g, conv_norm_b, conv_out_w, conv_out_b,
              qk_conv_w, qk_conv_b, mlstm_norm_g, mlstm_out_w, w_out, ln1_g, ln1_b,
              router_w, router_b, moe_w1, moe_b1, moe_w2, moe_b2, ln2_g, ln2_b):
    for l in range(DEPTH):
        x = hybrid_layer(x, w_in[l], b_in[l], conv_dw_w[l], conv_dw_b[l], conv_norm_g[l], conv_norm_b[l],
                         conv_out_w[l], conv_out_b[l], qk_conv_w[l], qk_conv_b[l], mlstm_norm_g[l],
                         mlstm_out_w[l], w_out[l], ln1_g[l], ln1_b[l], router_w[l], router_b[l],
                         moe_w1[l], moe_b1[l], moe_w2[l], moe_b2[l], ln2_g[l], ln2_b[l])
    return x
```

```python
import functools

import jax
import jax.numpy as jnp
from jax import lax
from jax.experimental import pallas as pl
from jax.experimental.pallas import tpu as pltpu

TOP_K = 4
SWIGLU_LIMIT = 7.0
SWIGLU_ALPHA = 1.702
LN_EPS = 1e-5
LANES_V7X = 128
SUBLANES_V7X = 8
VMEM_LIMIT_V7X = 56 * 1024 * 1024

F32 = jnp.float32
BF16 = jnp.bfloat16


def _cparams(sem):
    return pltpu.CompilerParams(dimension_semantics=sem, vmem_limit_bytes=VMEM_LIMIT_V7X)


def _ln_rows(x, g, b):
    mu = jnp.mean(x, axis=-1, keepdims=True)
    xc = x - mu
    var = jnp.mean(xc * xc, axis=-1, keepdims=True)
    y = xc * lax.rsqrt(var + LN_EPS) * g
    return y if b is None else y + b


def _sigmoid(x):
    return 1.0 / (1.0 + jnp.exp(-x))


def _log_sigmoid(x):
    return jnp.minimum(x, 0.0) - jnp.log(1.0 + jnp.exp(-jnp.abs(x)))


def _linear_kernel(x_ref, w_ref, b_ref, o_ref):
    o_ref[...] = jnp.dot(x_ref[...], w_ref[...], preferred_element_type=F32) + b_ref[...]


def _linear(x, w, b, tm, tn):
    T, K = x.shape
    N = w.shape[1]
    return pl.pallas_call(
        _linear_kernel,
        grid=(T // tm, N // tn),
        in_specs=[
            pl.BlockSpec((tm, K), lambda i, j: (i, 0)),
            pl.BlockSpec((K, tn), lambda i, j: (0, j)),
            pl.BlockSpec((1, tn), lambda i, j: (0, j)),
        ],
        out_specs=pl.BlockSpec((tm, tn), lambda i, j: (i, j)),
        out_shape=jax.ShapeDtypeStruct((T, N), F32),
        compiler_params=_cparams(("arbitrary", "arbitrary")),
        name="in_proj",
    )(x, w, b)


def _conformer_kernel(a_ref, g_ref, dww_ref, dwb_ref, ng_ref, nb_ref, ow_ref, ob_ref, y_ref,
                      zbuf, cbuf, *, taps, halo, rb, cb):
    s = pl.program_id(1)
    ts, C = a_ref.shape

    @pl.when(s == 0)
    def _():
        zbuf[0:halo, :] = jnp.zeros((halo, C), F32)

    @pl.when(s != 0)
    def _():
        zbuf[0:halo, :] = zbuf[ts:ts + halo, :]

    zbuf[halo:halo + ts, :] = a_ref[...] * _sigmoid(g_ref[...])
    off = halo - (taps - 1)
    for r0 in range(0, ts, rb):
        for c0 in range(0, C, cb):
            acc = jnp.broadcast_to(dwb_ref[:, c0:c0 + cb], (rb, cb))
            for j in range(taps):
                acc = acc + dww_ref[j:j + 1, c0:c0 + cb] * zbuf[r0 + off + j:r0 + off + j + rb, c0:c0 + cb]
            cbuf[r0:r0 + rb, c0:c0 + cb] = acc
    zn = _ln_rows(cbuf[...], ng_ref[...], nb_ref[...])
    act = zn * _sigmoid(zn)
    y_ref[...] = jnp.dot(act.astype(BF16), ow_ref[...], preferred_element_type=F32) + ob_ref[...]


def _conformer(xp, dww, dwb, ng, nb, ow, ob, B, S, ts):
    T = xp.shape[0]
    C = dww.shape[1]
    D = ow.shape[1]
    taps = dww.shape[0]
    halo = -(-(taps - 1) // SUBLANES_V7X) * SUBLANES_V7X
    nS = S // ts
    kern = functools.partial(_conformer_kernel, taps=taps, halo=halo, rb=min(32, ts), cb=min(256, C))
    const = lambda b, s: (0, 0)
    return pl.pallas_call(
        kern,
        grid=(B, nS),
        in_specs=[
            pl.BlockSpec((ts, C), lambda b, s: (b * nS + s, 0)),
            pl.BlockSpec((ts, C), lambda b, s: (b * nS + s, 1)),
            pl.BlockSpec((taps, C), const),
            pl.BlockSpec((1, C), const),
            pl.BlockSpec((1, C), const),
            pl.BlockSpec((1, C), const),
            pl.BlockSpec((C, D), const),
            pl.BlockSpec((1, D), const),
        ],
        out_specs=pl.BlockSpec((ts, D), lambda b, s: (b * nS + s, 0)),
        out_shape=jax.ShapeDtypeStruct((T, D), F32),
        scratch_shapes=[pltpu.VMEM((halo + ts, C), F32), pltpu.VMEM((ts, C), F32)],
        compiler_params=_cparams(("arbitrary", "arbitrary")),
        name="conformer",
    )(xp, xp, dww, dwb, ng, nb, ow, ob)


def _mlstm_kernel(q_ref, k_ref, v_ref, o_ref, x_ref, wif_ref, bif_ref, wq_ref, bq_ref, wk_ref, bk_ref,
                  ng_ref, ow_ref, y_ref, qbuf, kbuf, c_ref, n_ref, m_ref, hbuf, *, heads, taps, halo):
    s = pl.program_id(1)
    L, W = q_ref.shape
    dh = W // heads

    @pl.when(s == 0)
    def _():
        qbuf[0:halo, :] = jnp.zeros((halo, W), F32)
        kbuf[0:halo, :] = jnp.zeros((halo, W), F32)
        c_ref[...] = jnp.zeros(c_ref.shape, F32)
        n_ref[...] = jnp.zeros(n_ref.shape, F32)
        m_ref[...] = jnp.zeros(m_ref.shape, F32)

    @pl.when(s != 0)
    def _():
        qbuf[0:halo, :] = qbuf[L:L + halo, :]
        kbuf[0:halo, :] = kbuf[L:L + halo, :]

    qbuf[halo:halo + L, :] = q_ref[...]
    kbuf[halo:halo + L, :] = k_ref[...]
    off = halo - (taps - 1)

    def short_conv(buf, w_ref, b_ref):
        acc = b_ref[...] + w_ref[0:1, :] * buf[off:off + L, :]
        for j in range(1, taps):
            acc = acc + w_ref[j:j + 1, :] * buf[off + j:off + j + L, :]
        return acc * _sigmoid(acc)

    qc = short_conv(qbuf, wq_ref, bq_ref)
    kc = short_conv(kbuf, wk_ref, bk_ref) * (dh ** -0.5)

    gates = jnp.dot(x_ref[...], wif_ref[...], preferred_element_type=F32) + bif_ref[...]
    logf = _log_sigmoid(gates)
    row = lax.broadcasted_iota(jnp.int32, (L, L), 0)
    col = lax.broadcasted_iota(jnp.int32, (L, L), 1)
    causal = row >= col
    bcum = jnp.dot(causal.astype(F32), logf, preferred_element_type=F32, precision=lax.Precision.HIGHEST)
    gates_t = gates.T
    bcum_t = bcum.T

    for h in range(heads):
        sl = slice(h * dh, (h + 1) * dh)
        qf = qc[:, sl]
        kf = kc[:, sl]
        qh = qf.astype(BF16)
        kh = kf.astype(BF16)
        vf = v_ref[:, sl]
        ig_c = gates[:, h:h + 1]
        ig_r = gates_t[h:h + 1, :]
        b_c = bcum[:, heads + h:heads + h + 1]
        b_r = bcum_t[heads + h:heads + h + 1, :]
        m_prev = m_ref[h:h + 1, 0:1]
        dmat = jnp.where(causal, b_c - b_r + ig_r, -jnp.inf)
        inter = b_c + m_prev
        m_row = jnp.maximum(inter, jnp.max(dmat, axis=1, keepdims=True))
        w_intra = jnp.exp(dmat - m_row)
        w_inter = jnp.exp(inter - m_row)
        sc = lax.dot_general(qh, kh, (((1,), (1,)), ((), ())), preferred_element_type=F32) * w_intra
        cmat = c_ref[h]
        num = w_inter * jnp.dot(qh, cmat.astype(BF16), preferred_element_type=F32) + jnp.dot(
            sc.astype(BF16), vf.astype(BF16), preferred_element_type=F32)
        nvec = n_ref[h:h + 1, :]
        den = w_inter * jnp.sum(qf * nvec, axis=1, keepdims=True) + jnp.sum(sc, axis=1, keepdims=True)
        hh = num / jnp.maximum(jnp.abs(den), jnp.exp(-m_row))
        b_last = b_c[L - 1:L, :]
        lws_c = b_last - b_c + ig_c
        lws_r = b_last - b_r + ig_r
        m_new = jnp.maximum(b_last + m_prev, jnp.max(lws_r, axis=1, keepdims=True))
        ws_c = jnp.exp(lws_c - m_new)
        decay = jnp.exp(b_last + m_prev - m_new)
        kv = lax.dot_general(kh, (vf * ws_c).astype(BF16), (((0,), (0,)), ((), ())), preferred_element_type=F32)
        c_ref[h] = decay * cmat + kv
        n_ref[h:h + 1, :] = decay * nvec + jnp.sum(kf * ws_c, axis=0, keepdims=True)
        m_ref[h:h + 1, :] = jnp.broadcast_to(m_new, (1, m_ref.shape[1]))
        hn = _ln_rows(hh, ng_ref[:, sl], None)
        hbuf[:, sl] = hn * _sigmoid(o_ref[:, sl])
    y_ref[...] = jnp.dot(hbuf[...].astype(BF16), ow_ref[...], preferred_element_type=F32)


def _mlstm(xp, xb, wif, bif, wq, bq, wk, bk, ng, ow, B, S, L, heads, col_q):
    T = xp.shape[0]
    D = xb.shape[1]
    W = wq.shape[1]
    taps = wq.shape[0]
    dh = W // heads
    halo = SUBLANES_V7X
    nS = S // L
    kern = functools.partial(_mlstm_kernel, heads=heads, taps=taps, halo=halo)
    const = lambda b, s: (0, 0)

    def colspec(c):
        return pl.BlockSpec((L, W), lambda b, s: (b * nS + s, c))

    return pl.pallas_call(
        kern,
        grid=(B, nS),
        in_specs=[
            colspec(col_q), colspec(col_q + 1), colspec(col_q + 2), colspec(col_q + 3),
            pl.BlockSpec((L, D), lambda b, s: (b * nS + s, 0)),
            pl.BlockSpec(wif.shape, const),
            pl.BlockSpec(bif.shape, const),
            pl.BlockSpec((taps, W), const),
            pl.BlockSpec((1, W), const),
            pl.BlockSpec((taps, W), const),
            pl.BlockSpec((1, W), const),
            pl.BlockSpec((1, W), const),
            pl.BlockSpec((W, D), const),
        ],
        out_specs=pl.BlockSpec((L, D), lambda b, s: (b * nS + s, 0)),
        out_shape=jax.ShapeDtypeStruct((T, D), F32),
        scratch_shapes=[
            pltpu.VMEM((halo + L, W), F32),
            pltpu.VMEM((halo + L, W), F32),
            pltpu.VMEM((heads, dh, dh), F32),
            pltpu.VMEM((heads, dh), F32),
            pltpu.VMEM((heads, LANES_V7X), F32),
            pltpu.VMEM((L, W), F32),
        ],
        compiler_params=_cparams(("arbitrary", "arbitrary")),
        name="mlstm",
    )(xp, xp, xp, xp, xb, wif, bif, wq, bq, wk, bk, ng, ow)


def _merge_kernel(ga_ref, gb_ref, yc_ref, ym_ref, x_ref, wo_ref, lg_ref, lb_ref, rwt_ref, rb_ref,
                  x1_ref, idx_ref, gate_ref, rank_ref, cnt_ref, carry, *, alpha):
    i = pl.program_id(0)
    tm = x_ref.shape[0]
    E = rwt_ref.shape[0]

    @pl.when(i == 0)
    def _():
        carry[...] = jnp.zeros(carry.shape, F32)

    mix = _sigmoid(ga_ref[...]) * yc_ref[...] + _sigmoid(gb_ref[...]) * ym_ref[...]
    r = alpha * x_ref[...] + jnp.dot(mix.astype(BF16), wo_ref[...], preferred_element_type=F32)
    x1 = _ln_rows(r, lg_ref[...], lb_ref[...])
    x1_ref[...] = x1
    logits = lax.dot_general(rwt_ref[...], x1.astype(BF16), (((1,), (1,)), ((), ())),
                             preferred_element_type=F32) + rb_ref[:, 0:1]
    eidx = lax.broadcasted_iota(jnp.int32, (E, tm), 0).astype(F32)
    work = logits
    tops, hots = [], []
    for k in range(TOP_K):
        mk = jnp.max(work, axis=0, keepdims=True)
        ik = jnp.min(jnp.where(work == mk, eidx, float(E)), axis=0, keepdims=True)
        hot = eidx == ik
        work = jnp.where(hot, -jnp.inf, work)
        tops.append(mk)
        hots.append(hot)
        idx_ref[k:k + 1, :] = ik.astype(jnp.int32)
    exps = [jnp.exp(t - tops[0]) for t in tops]
    denom = exps[0]
    for e in exps[1:]:
        denom = denom + e
    for k in range(TOP_K):
        gate_ref[k:k + 1, :] = exps[k] / denom
    sel = hots[0].astype(F32)
    for hot in hots[1:]:
        sel = sel + hot.astype(F32)
    trow = lax.broadcasted_iota(jnp.int32, (tm, tm), 0)
    tcol = lax.broadcasted_iota(jnp.int32, (tm, tm), 1)
    before = (trow < tcol).astype(BF16)
    cum = jnp.dot(sel.astype(BF16), before, preferred_element_type=F32) + carry[:, 0:1]
    for k in range(TOP_K):
        rank_ref[k:k + 1, :] = jnp.sum(jnp.where(hots[k], cum, 0.0), axis=0, keepdims=True).astype(jnp.int32)
    total = carry[...] + jnp.sum(sel, axis=1, keepdims=True)
    carry[...] = total
    cnt_ref[...] = total


def _merge(xp, yc, ym, x, wo, lg, lb, rwt, rb, tm, col_ga, alpha):
    T, D = x.shape
    E = rwt.shape[0]
    kern = functools.partial(_merge_kernel, alpha=alpha)
    const = lambda i: (0, 0)
    row = lambda i: (i, 0)
    krow = pl.BlockSpec((TOP_K, tm), lambda i: (0, i))
    return pl.pallas_call(
        kern,
        grid=(T // tm,),
        in_specs=[
            pl.BlockSpec((tm, D), lambda i: (i, col_ga)),
            pl.BlockSpec((tm, D), lambda i: (i, col_ga + 1)),
            pl.BlockSpec((tm, D), row),
            pl.BlockSpec((tm, D), row),
            pl.BlockSpec((tm, D), row),
            pl.BlockSpec((D, D), const),
            pl.BlockSpec((1, D), const),
            pl.BlockSpec((1, D), const),
            pl.BlockSpec((E, D), const),
            pl.BlockSpec((E, LANES_V7X), const),
        ],
        out_specs=[
            pl.BlockSpec((tm, D), row),
            krow, krow, krow,
            pl.BlockSpec((E, LANES_V7X), const),
        ],
        out_shape=[
            jax.ShapeDtypeStruct((T, D), F32),
            jax.ShapeDtypeStruct((TOP_K, T), jnp.int32),
            jax.ShapeDtypeStruct((TOP_K, T), F32),
            jax.ShapeDtypeStruct((TOP_K, T), jnp.int32),
            jax.ShapeDtypeStruct((E, LANES_V7X), F32),
        ],
        scratch_shapes=[pltpu.VMEM((E, LANES_V7X), F32)],
        compiler_params=_cparams(("arbitrary",)),
        name="merge_router",
    )(xp, xp, yc, ym, x, wo, lg, lb, rwt, rb)


def _plan_kernel(cnt_ref, pstart_ref, blk_ref, *, blk):
    E = cnt_ref.shape[0]
    nbp = blk_ref.shape[1]
    cnt = cnt_ref[:, 0:1]
    padded = jnp.floor((cnt + (blk - 1)) / blk) * blk
    er = lax.broadcasted_iota(jnp.int32, (E, E), 0)
    ec = lax.broadcasted_iota(jnp.int32, (E, E), 1)
    padded_row = jnp.sum(jnp.where(er == ec, padded, 0.0), axis=0, keepdims=True)
    pstart = jnp.sum(jnp.where(ec < er, padded_row, 0.0), axis=1, keepdims=True)
    pend = pstart + padded
    pstart_ref[...] = jnp.broadcast_to(pstart, pstart_ref.shape)
    first_row = lax.broadcasted_iota(jnp.int32, (E, nbp), 1).astype(F32) * blk
    inside = jnp.logical_and(pstart <= first_row, first_row < pend)
    eid = lax.broadcasted_iota(jnp.int32, (E, nbp), 0).astype(F32)
    block_e = jnp.sum(jnp.where(inside, eid, 0.0), axis=0, keepdims=True)
    valid = jnp.sum(jnp.where(inside, 1.0, 0.0), axis=0, keepdims=True)
    e_last = jnp.max(jnp.where(padded > 0.0, eid[:, 0:1], 0.0), axis=0, keepdims=True)
    block_e = jnp.where(valid > 0.0, block_e, e_last)
    blk_ref[0:1, :] = block_e.astype(jnp.int32)
    blk_ref[1:2, :] = valid.astype(jnp.int32)
    blk_ref[2:SUBLANES_V7X, :] = jnp.zeros((SUBLANES_V7X - 2, nbp), jnp.int32)


def _plan(counts, blk, n_blocks):
    E = counts.shape[0]
    nbp = -(-n_blocks // LANES_V7X) * LANES_V7X
    return pl.pallas_call(
        functools.partial(_plan_kernel, blk=blk),
        out_shape=[
            jax.ShapeDtypeStruct((E, LANES_V7X), F32),
            jax.ShapeDtypeStruct((SUBLANES_V7X, nbp), jnp.int32),
        ],
        name="route_plan",
    )(counts)


def _dest_kernel(idx_ref, rank_ref, pstart_ref, dest_ref):
    E = pstart_ref.shape[0]
    tm = idx_ref.shape[1]
    eidx = lax.broadcasted_iota(jnp.int32, (E, tm), 0)
    pstart = pstart_ref[:, 0:1]
    for k in range(TOP_K):
        base = jnp.sum(jnp.where(eidx == idx_ref[k:k + 1, :], pstart, 0.0), axis=0, keepdims=True)
        dest_ref[k:k + 1, :] = base.astype(jnp.int32) + rank_ref[k:k + 1, :]


def _dest(idx, rank, pstart, tm):
    T = idx.shape[1]
    E = pstart.shape[0]
    krow = pl.BlockSpec((TOP_K, tm), lambda i: (0, i))
    return pl.pallas_call(
        _dest_kernel,
        grid=(T // tm,),
        in_specs=[krow, krow, pl.BlockSpec((E, LANES_V7X), lambda i: (0, 0))],
        out_specs=krow,
        out_shape=jax.ShapeDtypeStruct((TOP_K, T), jnp.int32),
        compiler_params=_cparams(("arbitrary",)),
        name="route_dest",
    )(idx, rank, pstart)


def _dispatch_kernel(dest_hbm, x_hbm, xs_in, xs_hbm, dsm, isem, rsem, *, td, T):
    del xs_in
    i = pl.program_id(0)
    for k in range(TOP_K):
        pltpu.make_async_copy(dest_hbm.at[pl.ds(k * T + i * td, td)], dsm.at[k], isem).start()
    for k in range(TOP_K):
        pltpu.make_async_copy(dest_hbm.at[pl.ds(k * T + i * td, td)], dsm.at[k], isem).wait()

    def issue(t, c):
        for k in range(TOP_K):
            pltpu.make_async_copy(x_hbm.at[pl.ds(i * td + t, 1)], xs_hbm.at[pl.ds(dsm[k, t], 1)], rsem).start()
        return c

    lax.fori_loop(0, td, issue, 0, unroll=8)
    for k in range(TOP_K):
        pltpu.make_async_copy(x_hbm.at[pl.ds(0, td)], xs_hbm.at[pl.ds(0, td)], rsem).wait()


def _dispatch(dest_flat, x1, xs_init, td):
    T, D = x1.shape
    return pl.pallas_call(
        functools.partial(_dispatch_kernel, td=td, T=T),
        grid=(T // td,),
        in_specs=[pl.BlockSpec(memory_space=pl.ANY)] * 3,
        out_specs=pl.BlockSpec(memory_space=pl.ANY),
        out_shape=jax.ShapeDtypeStruct(xs_init.shape, xs_init.dtype),
        scratch_shapes=[
            pltpu.SMEM((TOP_K, td), jnp.int32),
            pltpu.SemaphoreType.DMA,
            pltpu.SemaphoreType.DMA,
        ],
        input_output_aliases={2: 0},
        compiler_params=_cparams(("arbitrary",)),
        name="moe_dispatch",
    )(dest_flat, x1, xs_init)


def _expert_kernel(be_ref, bv_ref, xs_ref, w1_ref, b1_ref, w2_ref, b2_ref, y_ref):
    i = pl.program_id(0)
    F = w2_ref.shape[1]

    @pl.when(bv_ref[i] > 0)
    def _():
        h = jnp.dot(xs_ref[...].astype(BF16), w1_ref[0], preferred_element_type=F32) + b1_ref[0]
        g = jnp.minimum(h[:, :F], SWIGLU_LIMIT)
        lin = jnp.clip(h[:, F:], -SWIGLU_LIMIT, SWIGLU_LIMIT)
        act = g * _sigmoid(SWIGLU_ALPHA * g) * (lin + 1.0)
        y_ref[...] = jnp.dot(act.astype(BF16), w2_ref[0], preferred_element_type=F32) + b2_ref[0]

    @pl.when(bv_ref[i] <= 0)
    def _():
        y_ref[...] = jnp.zeros(y_ref.shape, F32)


def _experts(block_e, block_v, xs, w1, b1, w2, b2, blk):
    NR, D = xs.shape
    E, _, F2 = w1.shape
    F = w2.shape[1]
    wmap = lambda i, be, bv: (be[i], 0, 0)
    return pl.pallas_call(
        _expert_kernel,
        grid_spec=pltpu.PrefetchScalarGridSpec(
            num_scalar_prefetch=2,
            grid=(NR // blk,),
            in_specs=[
                pl.BlockSpec((blk, D), lambda i, be, bv: (i, 0)),
                pl.BlockSpec((1, D, F2), wmap),
                pl.BlockSpec((1, 1, F2), wmap),
                pl.BlockSpec((1, F, D), wmap),
                pl.BlockSpec((1, 1, D), wmap),
            ],
            out_specs=pl.BlockSpec((blk, D), lambda i, be, bv: (i, 0)),
        ),
        out_shape=jax.ShapeDtypeStruct((NR, D), F32),
        compiler_params=_cparams(("arbitrary",)),
        name="moe_experts",
    )(block_e, block_v, xs, w1, b1, w2, b2)


def _combine_kernel(dest_hbm, ys_hbm, x1_ref, gate_ref, lg_ref, lb_ref, o_ref, dsm, gbuf, isem, rsem,
                    *, tc, sub, T, alpha):
    i = pl.program_id(0)
    for k in range(TOP_K):
        pltpu.make_async_copy(dest_hbm.at[pl.ds(k * T + i * tc, tc)], dsm.at[k], isem).start()
    for k in range(TOP_K):
        pltpu.make_async_copy(dest_hbm.at[pl.ds(k * T + i * tc, tc)], dsm.at[k], isem).wait()

    for j in range(tc // sub):
        def issue(t, c):
            for k in range(TOP_K):
                pltpu.make_async_copy(ys_hbm.at[pl.ds(dsm[k, j * sub + t], 1)], gbuf.at[k, pl.ds(t, 1)], rsem).start()
            return c

        lax.fori_loop(0, sub, issue, 0, unroll=8)
        for k in range(TOP_K):
            pltpu.make_async_copy(ys_hbm.at[pl.ds(0, sub)], gbuf.at[k], rsem).wait()
        rows = slice(j * sub, (j + 1) * sub)
        gate = gate_ref[rows, :]
        y = gate[:, 0:1] * gbuf[0]
        for k in range(1, TOP_K):
            y = y + gate[:, k:k + 1] * gbuf[k]
        o_ref[rows, :] = _ln_rows(alpha * x1_ref[rows, :] + y, lg_ref[...], lb_ref[...])


def _combine(dest_flat, ys, x1, gate_t, lg, lb, tc, sub, alpha):
    T, D = x1.shape
    const = lambda i: (0, 0)
    return pl.pallas_call(
        functools.partial(_combine_kernel, tc=tc, sub=sub, T=T, alpha=alpha),
        grid=(T // tc,),
        in_specs=[
            pl.BlockSpec(memory_space=pl.ANY),
            pl.BlockSpec(memory_space=pl.ANY),
            pl.BlockSpec((tc, D), lambda i: (i, 0)),
            pl.BlockSpec((tc, TOP_K), lambda i: (i, 0)),
            pl.BlockSpec((1, D), const),
            pl.BlockSpec((1, D), const),
        ],
        out_specs=pl.BlockSpec((tc, D), lambda i: (i, 0)),
        out_shape=jax.ShapeDtypeStruct((T, D), F32),
        scratch_shapes=[
            pltpu.SMEM((TOP_K, tc), jnp.int32),
            pltpu.VMEM((TOP_K, sub, D), F32),
            pltpu.SemaphoreType.DMA,
            pltpu.SemaphoreType.DMA,
        ],
        compiler_params=_cparams(("arbitrary",)),
        name="moe_combine",
    )(dest_flat, ys, x1, gate_t, lg, lb)


def _tile(n, pref):
    t = min(n, pref)
    assert n % t == 0, (n, pref)
    return t


def kernel(x, w_in, b_in, conv_dw_w, conv_dw_b, conv_norm_g, conv_norm_b, conv_out_w, conv_out_b, qk_conv_w, qk_conv_b, mlstm_norm_g, mlstm_out_w, w_out, ln1_g, ln1_b, router_w, router_b, moe_w1, moe_b1, moe_w2, moe_b2, ln2_g, ln2_b):
    B, S, D = x.shape
    depth, _, n_in = w_in.shape
    C = conv_dw_w.shape[2]
    W = mlstm_out_w.shape[1]
    E = router_w.shape[2]
    heads = (n_in - 2 * C - 4 * W - 2 * D) // 2
    assert C == D and W == D and n_in == 2 * C + 4 * W + 2 * heads + 2 * D
    T = B * S
    alpha = float((2 * depth) ** 0.25)
    off_i = 2 * C + 4 * W
    n_main = n_in - 2 * heads

    tm_proj = _tile(T, 1024)
    ts_conv = _tile(S, 256)
    l_chunk = _tile(S, 256)
    tm_merge = _tile(T, 512)
    td = _tile(T, 1024)
    sub = _tile(td, 256)
    blk = _tile(T * TOP_K, 512)
    n_blocks = T * TOP_K // blk + E
    n_rows = n_blocks * blk

    w_main = jnp.concatenate([w_in[:, :, :off_i], w_in[:, :, off_i + 2 * heads:]], axis=2).astype(BF16)
    b_main = jnp.concatenate([b_in[:, :off_i], b_in[:, off_i + 2 * heads:]], axis=1)[:, None, :]
    w_if = jnp.pad(w_in[:, :, off_i:off_i + 2 * heads], ((0, 0), (0, 0), (0, LANES_V7X - 2 * heads))).astype(BF16)
    b_if = jnp.pad(b_in[:, off_i:off_i + 2 * heads], ((0, 0), (0, LANES_V7X - 2 * heads)))[:, None, :]
    conv_ow = conv_out_w.astype(BF16)
    ml_ow = mlstm_out_w.astype(BF16)
    wo = w_out.astype(BF16)
    rwt = jnp.swapaxes(router_w, 1, 2).astype(BF16)
    rb = jnp.broadcast_to(router_b[:, :, None], (depth, E, LANES_V7X))
    w1 = moe_w1.astype(BF16)
    w2 = moe_w2.astype(BF16)

    xf = x.reshape(T, D)
    for l in range(depth):
        xb = xf.astype(BF16)
        xp = _linear(xb, w_main[l], b_main[l], tm_proj, _tile(n_main, 1024))
        yc = _conformer(xp, conv_dw_w[l], conv_dw_b[l][None], conv_norm_g[l][None], conv_norm_b[l][None],
                        conv_ow[l], conv_out_b[l][None], B, S, ts_conv)
        ym = _mlstm(xp, xb, w_if[l], b_if[l], qk_conv_w[l][:, :W], qk_conv_b[l][None, :W],
                    qk_conv_w[l][:, W:], qk_conv_b[l][None, W:], mlstm_norm_g[l][None], ml_ow[l],
                    B, S, l_chunk, heads, 2 * C // W)
        x1, idx, gate, rank, counts = _merge(xp, yc, ym, xf, wo[l], ln1_g[l][None], ln1_b[l][None], rwt[l], rb[l],
                                             tm_merge, (2 * C + 4 * W) // D, alpha)
        pstart, blocks = _plan(counts, blk, n_blocks)
        dest = _dest(idx, rank, pstart, _tile(T, 2048)).reshape(TOP_K * T)
        xs = _dispatch(dest, x1, jnp.zeros((n_rows, D), F32), td)
        ys = _experts(blocks[0, :n_blocks], blocks[1, :n_blocks], xs, w1[l], moe_b1[l][:, None, :], w2[l],
                      moe_b2[l][:, None, :], blk)
        xf = _combine(dest, ys, x1, gate.T, ln2_g[l][None], ln2_b[l][None], td, sub, alpha)
    return xf.reshape(B, S, D)
```

```python
import functools

import jax
import jax.numpy as jnp
from jax import lax
from jax.experimental import pallas as pl
from jax.experimental.pallas import tpu as pltpu

TOP_K = 4
SWIGLU_LIMIT = 7.0
SWIGLU_ALPHA = 1.702
LN_EPS = 1e-5
LANES_V7X = 128
SUBLANES_V7X = 8
VMEM_LIMIT_V7X = 56 * 1024 * 1024

F32 = jnp.float32
BF16 = jnp.bfloat16


def _cparams(sem):
    return pltpu.CompilerParams(dimension_semantics=sem, vmem_limit_bytes=VMEM_LIMIT_V7X)


def _ln_rows(x, g, b):
    mu = jnp.mean(x, axis=-1, keepdims=True)
    xc = x - mu
    var = jnp.mean(xc * xc, axis=-1, keepdims=True)
    y = xc * lax.rsqrt(var + LN_EPS) * g
    return y if b is None else y + b


def _sigmoid(x):
    return 1.0 / (1.0 + jnp.exp(-x))


def _log_sigmoid(x):
    return jnp.minimum(x, 0.0) - jnp.log(1.0 + jnp.exp(-jnp.abs(x)))


def _linear_kernel(x_ref, w_ref, b_ref, o_ref):
    o_ref[...] = jnp.dot(x_ref[...], w_ref[...], preferred_element_type=F32) + b_ref[...]


def _linear(x, w, b, tm, tn):
    T, K = x.shape
    N = w.shape[1]
    return pl.pallas_call(
        _linear_kernel,
        grid=(T // tm, N // tn),
        in_specs=[
            pl.BlockSpec((tm, K), lambda i, j: (i, 0)),
            pl.BlockSpec((K, tn), lambda i, j: (0, j)),
            pl.BlockSpec((1, tn), lambda i, j: (0, j)),
        ],
        out_specs=pl.BlockSpec((tm, tn), lambda i, j: (i, j)),
        out_shape=jax.ShapeDtypeStruct((T, N), F32),
        compiler_params=_cparams(("arbitrary", "arbitrary")),
        name="in_proj",
    )(x, w, b)


def _conformer_kernel(a_ref, g_ref, dww_ref, dwb_ref, ng_ref, nb_ref, ow_ref, ob_ref, y_ref,
                      zbuf, cbuf, *, taps, halo, rb, cb):
    s = pl.program_id(1)
    ts, C = a_ref.shape

    @pl.when(s == 0)
    def _():
        zbuf[0:halo, :] = jnp.zeros((halo, C), F32)

    @pl.when(s != 0)
    def _():
        zbuf[0:halo, :] = zbuf[ts:ts + halo, :]

    zbuf[halo:halo + ts, :] = a_ref[...] * _sigmoid(g_ref[...])
    off = halo - (taps - 1)
    sub = SUBLANES_V7X
    for r0 in range(0, ts, rb):
        for c0 in range(0, C, cb):
            acc = jnp.broadcast_to(dwb_ref[:, c0:c0 + cb], (rb, cb))
            for r in range(sub):
                wl = rb + (sub if r else 0)
                q = None
                for a in range((off + taps - 1) // sub + 1):
                    j = sub * a + r - off
                    if 0 <= j < taps:
                        assert r0 + sub * a + wl <= halo + ts
                        term = dww_ref[j:j + 1, c0:c0 + cb] * zbuf[r0 + sub * a:r0 + sub * a + wl, c0:c0 + cb]
                        q = term if q is None else q + term
                if q is not None:
                    acc = acc + q[r:r + rb, :]
            cbuf[r0:r0 + rb, c0:c0 + cb] = acc
    zn = _ln_rows(cbuf[...], ng_ref[...], nb_ref[...])
    act = zn * _sigmoid(zn)
    y_ref[...] = jnp.dot(act.astype(BF16), ow_ref[...], preferred_element_type=F32) + ob_ref[...]


def _conformer(xp, dww, dwb, ng, nb, ow, ob, B, S, ts):
    T = xp.shape[0]
    C = dww.shape[1]
    D = ow.shape[1]
    taps = dww.shape[0]
    halo = -(-(taps - 1) // SUBLANES_V7X) * SUBLANES_V7X
    nS = S // ts
    kern = functools.partial(_conformer_kernel, taps=taps, halo=halo, rb=min(64, ts), cb=min(256, C))
    const = lambda b, s: (0, 0)
    return pl.pallas_call(
        kern,
        grid=(B, nS),
        in_specs=[
            pl.BlockSpec((ts, C), lambda b, s: (b * nS + s, 0)),
            pl.BlockSpec((ts, C), lambda b, s: (b * nS + s, 1)),
            pl.BlockSpec((taps, C), const),
            pl.BlockSpec((1, C), const),
            pl.BlockSpec((1, C), const),
            pl.BlockSpec((1, C), const),
            pl.BlockSpec((C, D), const),
            pl.BlockSpec((1, D), const),
        ],
        out_specs=pl.BlockSpec((ts, D), lambda b, s: (b * nS + s, 0)),
        out_shape=jax.ShapeDtypeStruct((T, D), F32),
        scratch_shapes=[pltpu.VMEM((halo + ts, C), F32), pltpu.VMEM((ts, C), F32)],
        compiler_params=_cparams(("arbitrary", "arbitrary")),
        name="conformer",
    )(xp, xp, dww, dwb, ng, nb, ow, ob)


def _mlstm_kernel(q_ref, k_ref, v_ref, o_ref, x_ref, wif_ref, bif_ref, wq_ref, bq_ref, wk_ref, bk_ref,
                  ng_ref, ow_ref, y_ref, qbuf, kbuf, c_ref, n_ref, m_ref, hbuf, *, heads, taps, halo):
    s = pl.program_id(1)
    L, W = q_ref.shape
    dh = W // heads

    @pl.when(s == 0)
    def _():
        qbuf[0:halo, :] = jnp.zeros((halo, W), F32)
        kbuf[0:halo, :] = jnp.zeros((halo, W), F32)
        c_ref[...] = jnp.zeros(c_ref.shape, F32)
        n_ref[...] = jnp.zeros(n_ref.shape, F32)
        m_ref[...] = jnp.zeros(m_ref.shape, F32)

    @pl.when(s != 0)
    def _():
        qbuf[0:halo, :] = qbuf[L:L + halo, :]
        kbuf[0:halo, :] = kbuf[L:L + halo, :]

    qbuf[halo:halo + L, :] = q_ref[...]
    kbuf[halo:halo + L, :] = k_ref[...]
    off = halo - (taps - 1)

    def short_conv(buf, w_ref, b_ref):
        acc = b_ref[...] + w_ref[0:1, :] * buf[off:off + L, :]
        for j in range(1, taps):
            acc = acc + w_ref[j:j + 1, :] * buf[off + j:off + j + L, :]
        return acc * _sigmoid(acc)

    qc = short_conv(qbuf, wq_ref, bq_ref)
    kc = short_conv(kbuf, wk_ref, bk_ref) * (dh ** -0.5)

    gates = jnp.dot(x_ref[...], wif_ref[...], preferred_element_type=F32) + bif_ref[...]
    logf = _log_sigmoid(gates)
    row = lax.broadcasted_iota(jnp.int32, (L, L), 0)
    col = lax.broadcasted_iota(jnp.int32, (L, L), 1)
    causal = row >= col
    bcum = jnp.dot(causal.astype(F32), logf, preferred_element_type=F32, precision=lax.Precision.HIGHEST)
    gates_t = gates.T
    bcum_t = bcum.T

    for h in range(heads):
        sl = slice(h * dh, (h + 1) * dh)
        qf = qc[:, sl]
        kf = kc[:, sl]
        qh = qf.astype(BF16)
        kh = kf.astype(BF16)
        vf = v_ref[:, sl]
        ig_c = gates[:, h:h + 1]
        ig_r = gates_t[h:h + 1, :]
        b_c = bcum[:, heads + h:heads + h + 1]
        b_r = bcum_t[heads + h:heads + h + 1, :]
        m_prev = m_ref[h:h + 1, 0:1]
        dmat = jnp.where(causal, b_c - b_r + ig_r, -jnp.inf)
        inter = b_c + m_prev
        m_row = jnp.maximum(inter, jnp.max(dmat, axis=1, keepdims=True))
        w_intra = jnp.exp(dmat - m_row)
        w_inter = jnp.exp(inter - m_row)
        sc = lax.dot_general(qh, kh, (((1,), (1,)), ((), ())), preferred_element_type=F32) * w_intra
        cmat = c_ref[h]
        num = w_inter * jnp.dot(qh, cmat.astype(BF16), preferred_element_type=F32) + jnp.dot(
            sc.astype(BF16), vf.astype(BF16), preferred_element_type=F32)
        nvec = n_ref[h:h + 1, :]
        den = w_inter * jnp.sum(qf * nvec, axis=1, keepdims=True) + jnp.sum(sc, axis=1, keepdims=True)
        hh = num / jnp.maximum(jnp.abs(den), jnp.exp(-m_row))
        b_last = b_c[L - 1:L, :]
        lws_c = b_last - b_c + ig_c
        lws_r = b_last - b_r + ig_r
        m_new = jnp.maximum(b_last + m_prev, jnp.max(lws_r, axis=1, keepdims=True))
        ws_c = jnp.exp(lws_c - m_new)
        decay = jnp.exp(b_last + m_prev - m_new)
        kv = lax.dot_general(kh, (vf * ws_c).astype(BF16), (((0,), (0,)), ((), ())), preferred_element_type=F32)
        c_ref[h] = decay * cmat + kv
        n_ref[h:h + 1, :] = decay * nvec + jnp.sum(kf * ws_c, axis=0, keepdims=True)
        m_ref[h:h + 1, :] = jnp.broadcast_to(m_new, (1, m_ref.shape[1]))
        hn = _ln_rows(hh, ng_ref[:, sl], None)
        hbuf[:, sl] = hn * _sigmoid(o_ref[:, sl])
    y_ref[...] = jnp.dot(hbuf[...].astype(BF16), ow_ref[...], preferred_element_type=F32)


def _mlstm(xp, xb, wif, bif, wq, bq, wk, bk, ng, ow, B, S, L, heads, col_q):
    T = xp.shape[0]
    D = xb.shape[1]
    W = wq.shape[1]
    taps = wq.shape[0]
    dh = W // heads
    halo = SUBLANES_V7X
    nS = S // L
    kern = functools.partial(_mlstm_kernel, heads=heads, taps=taps, halo=halo)
    const = lambda b, s: (0, 0)

    def colspec(c):
        return pl.BlockSpec((L, W), lambda b, s: (b * nS + s, c))

    return pl.pallas_call(
        kern,
        grid=(B, nS),
        in_specs=[
            colspec(col_q), colspec(col_q + 1), colspec(col_q + 2), colspec(col_q + 3),
            pl.BlockSpec((L, D), lambda b, s: (b * nS + s, 0)),
            pl.BlockSpec(wif.shape, const),
            pl.BlockSpec(bif.shape, const),
            pl.BlockSpec((taps, W), const),
            pl.BlockSpec((1, W), const),
            pl.BlockSpec((taps, W), const),
            pl.BlockSpec((1, W), const),
            pl.BlockSpec((1, W), const),
            pl.BlockSpec((W, D), const),
        ],
        out_specs=pl.BlockSpec((L, D), lambda b, s: (b * nS + s, 0)),
        out_shape=jax.ShapeDtypeStruct((T, D), F32),
        scratch_shapes=[
            pltpu.VMEM((halo + L, W), F32),
            pltpu.VMEM((halo + L, W), F32),
            pltpu.VMEM((heads, dh, dh), F32),
            pltpu.VMEM((heads, dh), F32),
            pltpu.VMEM((heads, LANES_V7X), F32),
            pltpu.VMEM((L, W), F32),
        ],
        compiler_params=_cparams(("arbitrary", "arbitrary")),
        name="mlstm",
    )(xp, xp, xp, xp, xb, wif, bif, wq, bq, wk, bk, ng, ow)


def _merge_kernel(ga_ref, gb_ref, yc_ref, ym_ref, x_ref, wo_ref, lg_ref, lb_ref, rwt_ref, rb_ref,
                  x1_ref, idx_ref, gate_ref, rank_ref, cnt_ref, carry, *, alpha):
    i = pl.program_id(0)
    tm = x_ref.shape[0]
    E = rwt_ref.shape[0]

    @pl.when(i == 0)
    def _():
        carry[...] = jnp.zeros(carry.shape, F32)

    mix = _sigmoid(ga_ref[...]) * yc_ref[...] + _sigmoid(gb_ref[...]) * ym_ref[...]
    r = alpha * x_ref[...] + jnp.dot(mix.astype(BF16), wo_ref[...], preferred_element_type=F32)
    x1 = _ln_rows(r, lg_ref[...], lb_ref[...])
    x1_ref[...] = x1
    logits = lax.dot_general(rwt_ref[...], x1.astype(BF16), (((1,), (1,)), ((), ())),
                             preferred_element_type=F32) + rb_ref[:, 0:1]
    eidx = lax.broadcasted_iota(jnp.int32, (E, tm), 0).astype(F32)
    work = logits
    tops, hots = [], []
    for k in range(TOP_K):
        mk = jnp.max(work, axis=0, keepdims=True)
        ik = jnp.min(jnp.where(work == mk, eidx, float(E)), axis=0, keepdims=True)
        hot = eidx == ik
        work = jnp.where(hot, -jnp.inf, work)
        tops.append(mk)
        hots.append(hot)
        idx_ref[k:k + 1, :] = ik.astype(jnp.int32)
    exps = [jnp.exp(t - tops[0]) for t in tops]
    denom = exps[0]
    for e in exps[1:]:
        denom = denom + e
    for k in range(TOP_K):
        gate_ref[k:k + 1, :] = exps[k] / denom
    sel = hots[0].astype(F32)
    for hot in hots[1:]:
        sel = sel + hot.astype(F32)
    trow = lax.broadcasted_iota(jnp.int32, (tm, tm), 0)
    tcol = lax.broadcasted_iota(jnp.int32, (tm, tm), 1)
    before = (trow < tcol).astype(BF16)
    cum = jnp.dot(sel.astype(BF16), before, preferred_element_type=F32) + carry[:, 0:1]
    for k in range(TOP_K):
        rank_ref[k:k + 1, :] = jnp.sum(jnp.where(hots[k], cum, 0.0), axis=0, keepdims=True).astype(jnp.int32)
    total = carry[...] + jnp.sum(sel, axis=1, keepdims=True)
    carry[...] = total
    cnt_ref[...] = total


def _merge(xp, yc, ym, x, wo, lg, lb, rwt, rb, tm, col_ga, alpha):
    T, D = x.shape
    E = rwt.shape[0]
    kern = functools.partial(_merge_kernel, alpha=alpha)
    const = lambda i: (0, 0)
    row = lambda i: (i, 0)
    krow = pl.BlockSpec((TOP_K, tm), lambda i: (0, i))
    return pl.pallas_call(
        kern,
        grid=(T // tm,),
        in_specs=[
            pl.BlockSpec((tm, D), lambda i: (i, col_ga)),
            pl.BlockSpec((tm, D), lambda i: (i, col_ga + 1)),
            pl.BlockSpec((tm, D), row),
            pl.BlockSpec((tm, D), row),
            pl.BlockSpec((tm, D), row),
            pl.BlockSpec((D, D), const),
            pl.BlockSpec((1, D), const),
            pl.BlockSpec((1, D), const),
            pl.BlockSpec((E, D), const),
            pl.BlockSpec((E, LANES_V7X), const),
        ],
        out_specs=[
            pl.BlockSpec((tm, D), row),
            krow, krow, krow,
            pl.BlockSpec((E, LANES_V7X), const),
        ],
        out_shape=[
            jax.ShapeDtypeStruct((T, D), F32),
            jax.ShapeDtypeStruct((TOP_K, T), jnp.int32),
            jax.ShapeDtypeStruct((TOP_K, T), F32),
            jax.ShapeDtypeStruct((TOP_K, T), jnp.int32),
            jax.ShapeDtypeStruct((E, LANES_V7X), F32),
        ],
        scratch_shapes=[pltpu.VMEM((E, LANES_V7X), F32)],
        compiler_params=_cparams(("arbitrary",)),
        name="merge_router",
    )(xp, xp, yc, ym, x, wo, lg, lb, rwt, rb)


def _plan_kernel(cnt_ref, pstart_ref, blk_ref, meta_ref, *, blk):
    E = cnt_ref.shape[0]
    nbp = blk_ref.shape[1]
    cnt = cnt_ref[:, 0:1]
    padded = jnp.floor((cnt + (blk - 1)) / blk) * blk
    er = lax.broadcasted_iota(jnp.int32, (E, E), 0)
    ec = lax.broadcasted_iota(jnp.int32, (E, E), 1)
    padded_row = jnp.sum(jnp.where(er == ec, padded, 0.0), axis=0, keepdims=True)
    pstart = jnp.sum(jnp.where(ec < er, padded_row, 0.0), axis=1, keepdims=True)
    pend = pstart + padded
    pstart_ref[...] = jnp.broadcast_to(pstart, pstart_ref.shape)
    pend_row = jnp.sum(jnp.where(er <= ec, padded, 0.0), axis=0, keepdims=True)
    n_used = jnp.sum(padded_row, axis=1, keepdims=True) / blk
    meta_ref[...] = jnp.broadcast_to(n_used, meta_ref.shape).astype(jnp.int32)
    meta_ref[0:1, 0:E] = pend_row.astype(jnp.int32)
    first_row = lax.broadcasted_iota(jnp.int32, (E, nbp), 1).astype(F32) * blk
    inside = jnp.logical_and(pstart <= first_row, first_row < pend)
    eid = lax.broadcasted_iota(jnp.int32, (E, nbp), 0).astype(F32)
    block_e = jnp.sum(jnp.where(inside, eid, 0.0), axis=0, keepdims=True)
    valid = jnp.sum(jnp.where(inside, 1.0, 0.0), axis=0, keepdims=True)
    e_last = jnp.max(jnp.where(padded > 0.0, eid[:, 0:1], 0.0), axis=0, keepdims=True)
    block_e = jnp.where(valid > 0.0, block_e, e_last)
    blk_ref[...] = jnp.broadcast_to(block_e.astype(jnp.int32), blk_ref.shape)


def _plan(counts, blk, n_blocks):
    E = counts.shape[0]
    nbp = -(-n_blocks // LANES_V7X) * LANES_V7X
    return pl.pallas_call(
        functools.partial(_plan_kernel, blk=blk),
        out_shape=[
            jax.ShapeDtypeStruct((E, LANES_V7X), F32),
            jax.ShapeDtypeStruct((SUBLANES_V7X, nbp), jnp.int32),
            jax.ShapeDtypeStruct((SUBLANES_V7X, LANES_V7X), jnp.int32),
        ],
        name="route_plan",
    )(counts)


def _dest_kernel(idx_ref, rank_ref, pstart_ref, dest_ref):
    E = pstart_ref.shape[0]
    tm = idx_ref.shape[1]
    eidx = lax.broadcasted_iota(jnp.int32, (E, tm), 0)
    pstart = pstart_ref[:, 0:1]
    for k in range(TOP_K):
        base = jnp.sum(jnp.where(eidx == idx_ref[k:k + 1, :], pstart, 0.0), axis=0, keepdims=True)
        dest_ref[k:k + 1, :] = base.astype(jnp.int32) + rank_ref[k:k + 1, :]


def _dest(idx, rank, pstart, tm):
    T = idx.shape[1]
    E = pstart.shape[0]
    krow = pl.BlockSpec((TOP_K, tm), lambda i: (0, i))
    return pl.pallas_call(
        _dest_kernel,
        grid=(T // tm,),
        in_specs=[krow, krow, pl.BlockSpec((E, LANES_V7X), lambda i: (0, 0))],
        out_specs=krow,
        out_shape=jax.ShapeDtypeStruct((TOP_K, T), jnp.int32),
        compiler_params=_cparams(("arbitrary",)),
        name="route_dest",
    )(idx, rank, pstart)


def _dispatch_kernel(pend_ref, dest_hbm, x_ref, xs_hbm, dsm, zbuf, isem, rsem, zsem, *, td, T, blk, n_blocks):
    i = pl.program_id(0)
    E = pend_ref.shape[0] - 1
    idx_copies = [pltpu.make_async_copy(dest_hbm.at[pl.ds(k * T + i * td, td)], dsm.at[k], isem)
                  for k in range(TOP_K)]
    for c in idx_copies:
        c.start()

    @pl.when(i == 0)
    def _():
        zbuf[...] = jnp.zeros(zbuf.shape, zbuf.dtype)
        n_used = pend_ref[E]

        def clear(start):
            return pltpu.make_async_copy(zbuf, xs_hbm.at[pl.ds(pl.multiple_of(start, blk), blk)], zsem)

        def clears(fn):
            for e in range(E):
                @pl.when(pend_ref[e] >= blk)
                def _():
                    fn(clear(pend_ref[e] - blk))

                @pl.when(n_used + e < n_blocks)
                def _():
                    fn(clear((n_used + e) * blk))

        clears(lambda c: c.start())
        clears(lambda c: c.wait())

    for c in idx_copies:
        c.wait()

    def issue(t, c):
        for k in range(TOP_K):
            pltpu.make_async_copy(x_ref.at[pl.ds(t, 1)], xs_hbm.at[pl.ds(dsm[k, t], 1)], rsem).start()
        return c

    lax.fori_loop(0, td, issue, 0, unroll=8)
    for k in range(TOP_K):
        pltpu.make_async_copy(x_ref, xs_hbm.at[pl.ds(0, td)], rsem).wait()


def _dispatch(pend, dest_flat, x1, n_rows, td, blk):
    T, D = x1.shape
    return pl.pallas_call(
        functools.partial(_dispatch_kernel, td=td, T=T, blk=blk, n_blocks=n_rows // blk),
        grid_spec=pltpu.PrefetchScalarGridSpec(
            num_scalar_prefetch=1,
            grid=(T // td,),
            in_specs=[
                pl.BlockSpec(memory_space=pl.ANY),
                pl.BlockSpec((td, D), lambda i, pend: (i, 0)),
            ],
            out_specs=pl.BlockSpec(memory_space=pl.ANY),
            scratch_shapes=[
                pltpu.SMEM((TOP_K, td), jnp.int32),
                pltpu.VMEM((blk, D), F32),
                pltpu.SemaphoreType.DMA,
                pltpu.SemaphoreType.DMA,
                pltpu.SemaphoreType.DMA,
            ],
        ),
        out_shape=jax.ShapeDtypeStruct((n_rows, D), F32),
        compiler_params=_cparams(("arbitrary",)),
        name="moe_dispatch",
    )(pend, dest_flat, x1)


def _expert_kernel(be_ref, nu_ref, xs_ref, w1_ref, b1_ref, w2_ref, b2_ref, y_ref, w1b, w2b):
    i = pl.program_id(0)
    F = w2_ref.shape[1]

    @pl.when(jnp.logical_or(i == 0, be_ref[i] != be_ref[jnp.maximum(i - 1, 0)]))
    def _():
        w1b[...] = w1_ref[0].astype(BF16)
        w2b[...] = w2_ref[0].astype(BF16)

    @pl.when(i < nu_ref[0])
    def _():
        h = jnp.dot(xs_ref[...].astype(BF16), w1b[...], preferred_element_type=F32) + b1_ref[0]
        g = jnp.minimum(h[:, :F], SWIGLU_LIMIT)
        lin = jnp.clip(h[:, F:], -SWIGLU_LIMIT, SWIGLU_LIMIT)
        act = g * _sigmoid(SWIGLU_ALPHA * g) * (lin + 1.0)
        y_ref[...] = jnp.dot(act.astype(BF16), w2b[...], preferred_element_type=F32) + b2_ref[0]

    @pl.when(i >= nu_ref[0])
    def _():
        y_ref[...] = jnp.zeros(y_ref.shape, F32)


def _experts(block_e, n_used, xs, w1, b1, w2, b2, blk):
    NR, D = xs.shape
    E, _, F2 = w1.shape
    F = w2.shape[1]
    wmap = lambda i, be, nu: (be[i], 0, 0)
    return pl.pallas_call(
        _expert_kernel,
        grid_spec=pltpu.PrefetchScalarGridSpec(
            num_scalar_prefetch=2,
            grid=(NR // blk,),
            in_specs=[
                pl.BlockSpec((blk, D), lambda i, be, nu: (jnp.minimum(i, nu[0] - 1), 0)),
                pl.BlockSpec((1, D, F2), wmap),
                pl.BlockSpec((1, 1, F2), wmap),
                pl.BlockSpec((1, F, D), wmap),
                pl.BlockSpec((1, 1, D), wmap),
            ],
            out_specs=pl.BlockSpec((blk, D), lambda i, be, nu: (i, 0)),
            scratch_shapes=[pltpu.VMEM((D, F2), BF16), pltpu.VMEM((F, D), BF16)],
        ),
        out_shape=jax.ShapeDtypeStruct((NR, D), F32),
        compiler_params=_cparams(("arbitrary",)),
        name="moe_experts",
    )(block_e, n_used, xs, w1, b1, w2, b2)


def _combine_kernel(dest_hbm, ys_hbm, x1_ref, gate_ref, lg_ref, lb_ref, o_ref, ob_ref, dsm, gbuf, isem, rsem,
                    *, tc, sub, T, alpha):
    i = pl.program_id(0)
    for k in range(TOP_K):
        pltpu.make_async_copy(dest_hbm.at[pl.ds(k * T + i * tc, tc)], dsm.at[k], isem).start()
    for k in range(TOP_K):
        pltpu.make_async_copy(dest_hbm.at[pl.ds(k * T + i * tc, tc)], dsm.at[k], isem).wait()

    for j in range(tc // sub):
        def issue(t, c):
            for k in range(TOP_K):
                pltpu.make_async_copy(ys_hbm.at[pl.ds(dsm[k, j * sub + t], 1)], gbuf.at[k, pl.ds(t, 1)], rsem).start()
            return c

        lax.fori_loop(0, sub, issue, 0, unroll=8)
        for k in range(TOP_K):
            pltpu.make_async_copy(ys_hbm.at[pl.ds(0, sub)], gbuf.at[k], rsem).wait()
        rows = slice(j * sub, (j + 1) * sub)
        gate = gate_ref[rows, :]
        y = gate[:, 0:1] * gbuf[0]
        for k in range(1, TOP_K):
            y = y + gate[:, k:k + 1] * gbuf[k]
        out = _ln_rows(alpha * x1_ref[rows, :] + y, lg_ref[...], lb_ref[...])
        o_ref[rows, :] = out
        ob_ref[rows, :] = out.astype(BF16)


def _combine(dest_flat, ys, x1, gate_t, lg, lb, tc, sub, alpha):
    T, D = x1.shape
    const = lambda i: (0, 0)
    return pl.pallas_call(
        functools.partial(_combine_kernel, tc=tc, sub=sub, T=T, alpha=alpha),
        grid=(T // tc,),
        in_specs=[
            pl.BlockSpec(memory_space=pl.ANY),
            pl.BlockSpec(memory_space=pl.ANY),
            pl.BlockSpec((tc, D), lambda i: (i, 0)),
            pl.BlockSpec((tc, TOP_K), lambda i: (i, 0)),
            pl.BlockSpec((1, D), const),
            pl.BlockSpec((1, D), const),
        ],
        out_specs=[pl.BlockSpec((tc, D), lambda i: (i, 0)), pl.BlockSpec((tc, D), lambda i: (i, 0))],
        out_shape=[jax.ShapeDtypeStruct((T, D), F32), jax.ShapeDtypeStruct((T, D), BF16)],
        scratch_shapes=[
            pltpu.SMEM((TOP_K, tc), jnp.int32),
            pltpu.VMEM((TOP_K, sub, D), F32),
            pltpu.SemaphoreType.DMA,
            pltpu.SemaphoreType.DMA,
        ],
        compiler_params=_cparams(("arbitrary",)),
        name="moe_combine",
    )(dest_flat, ys, x1, gate_t, lg, lb)


def _tile(n, pref):
    t = min(n, pref)
    assert n % t == 0, (n, pref)
    return t


def kernel(x, w_in, b_in, conv_dw_w, conv_dw_b, conv_norm_g, conv_norm_b, conv_out_w, conv_out_b, qk_conv_w, qk_conv_b, mlstm_norm_g, mlstm_out_w, w_out, ln1_g, ln1_b, router_w, router_b, moe_w1, moe_b1, moe_w2, moe_b2, ln2_g, ln2_b):
    B, S, D = x.shape
    depth, _, n_in = w_in.shape
    C = conv_dw_w.shape[2]
    W = mlstm_out_w.shape[1]
    E = router_w.shape[2]
    heads = (n_in - 2 * C - 4 * W - 2 * D) // 2
    assert C == D and W == D and n_in == 2 * C + 4 * W + 2 * heads + 2 * D
    T = B * S
    alpha = float((2 * depth) ** 0.25)
    off_i = 2 * C + 4 * W
    n_main = n_in - 2 * heads

    tm_proj = _tile(T, 1024)
    ts_conv = _tile(S, 256)
    l_chunk = _tile(S, 256)
    tm_merge = _tile(T, 512)
    td = _tile(T, 1024)
    sub = _tile(td, 256)
    blk = _tile(T * TOP_K, 512)
    n_blocks = T * TOP_K // blk + E
    n_rows = n_blocks * blk

    w_main = jnp.concatenate([w_in[:, :, :off_i], w_in[:, :, off_i + 2 * heads:]], axis=2).astype(BF16)
    b_main = jnp.concatenate([b_in[:, :off_i], b_in[:, off_i + 2 * heads:]], axis=1)[:, None, :]
    w_if = jnp.pad(w_in[:, :, off_i:off_i + 2 * heads], ((0, 0), (0, 0), (0, LANES_V7X - 2 * heads))).astype(BF16)
    b_if = jnp.pad(b_in[:, off_i:off_i + 2 * heads], ((0, 0), (0, LANES_V7X - 2 * heads)))[:, None, :]
    conv_ow = conv_out_w.astype(BF16)
    ml_ow = mlstm_out_w.astype(BF16)
    wo = w_out.astype(BF16)
    rwt = jnp.swapaxes(router_w, 1, 2).astype(BF16)
    rb = jnp.broadcast_to(router_b[:, :, None], (depth, E, LANES_V7X))

    xf = x.reshape(T, D)
    xb = xf.astype(BF16)
    for l in range(depth):
        xp = _linear(xb, w_main[l], b_main[l], tm_proj, _tile(n_main, 1024))
        yc = _conformer(xp, conv_dw_w[l], conv_dw_b[l][None], conv_norm_g[l][None], conv_norm_b[l][None],
                        conv_ow[l], conv_out_b[l][None], B, S, ts_conv)
        ym = _mlstm(xp, xb, w_if[l], b_if[l], qk_conv_w[l][:, :W], qk_conv_b[l][None, :W],
                    qk_conv_w[l][:, W:], qk_conv_b[l][None, W:], mlstm_norm_g[l][None], ml_ow[l],
                    B, S, l_chunk, heads, 2 * C // W)
        x1, idx, gate, rank, counts = _merge(xp, yc, ym, xf, wo[l], ln1_g[l][None], ln1_b[l][None], rwt[l], rb[l],
                                             tm_merge, (2 * C + 4 * W) // D, alpha)
        pstart, blocks, meta = _plan(counts, blk, n_blocks)
        dest = _dest(idx, rank, pstart, _tile(T, 2048)).reshape(TOP_K * T)
        xs = _dispatch(meta[0, :E + 1], dest, x1, n_rows, td, blk)
        ys = _experts(blocks[0, :n_blocks], meta[1, :1], xs, moe_w1[l], moe_b1[l][:, None, :], moe_w2[l],
                      moe_b2[l][:, None, :], blk)
        xf, xb = _combine(dest, ys, x1, gate.T, ln2_g[l][None], ln2_b[l][None], td, sub, alpha)
    return xf.reshape(B, S, D)
```

```python
import functools

import jax
import jax.numpy as jnp
from jax import lax
from jax.experimental import pallas as pl
from jax.experimental.pallas import tpu as pltpu

TOP_K = 4
SWIGLU_LIMIT = 7.0
SWIGLU_ALPHA = 1.702
LN_EPS = 1e-5
LANES_V7X = 128
SUBLANES_V7X = 8
VMEM_LIMIT_V7X = 56 * 1024 * 1024

F32 = jnp.float32
BF16 = jnp.bfloat16


def _cparams(sem):
    return pltpu.CompilerParams(dimension_semantics=sem, vmem_limit_bytes=VMEM_LIMIT_V7X)


def _ln_rows(x, g, b):
    mu = jnp.mean(x, axis=-1, keepdims=True)
    xc = x - mu
    var = jnp.mean(xc * xc, axis=-1, keepdims=True)
    y = xc * lax.rsqrt(var + LN_EPS) * g
    return y if b is None else y + b


def _sigmoid(x):
    return 1.0 / (1.0 + jnp.exp(-x))


def _log_sigmoid(x):
    return jnp.minimum(x, 0.0) - jnp.log(1.0 + jnp.exp(-jnp.abs(x)))


def _rows_from_tiles(ref, n):
    r = ref.shape[0] // n
    return jnp.concatenate([ref[pl.ds(s, n, stride=r), :] for s in range(r)], axis=1)


def _rows_to_tiles(ref, val):
    n = val.shape[0]
    r = ref.shape[0] // n
    for s in range(r):
        ref[pl.ds(s, n, stride=r), :] = val[:, s * LANES_V7X:(s + 1) * LANES_V7X]


def _linear_kernel(x_ref, wa_ref, wb_ref, b_ref, o_ref, *, na_blocks):
    j = pl.program_id(1)

    @pl.when(j < na_blocks)
    def _():
        o_ref[...] = (jnp.dot(x_ref[...], wa_ref[...], preferred_element_type=F32) + b_ref[...]).astype(o_ref.dtype)

    @pl.when(j >= na_blocks)
    def _():
        o_ref[...] = (jnp.dot(x_ref[...], wb_ref[...], preferred_element_type=F32) + b_ref[...]).astype(o_ref.dtype)


def _linear(x, wa, wb, b, tm, tn):
    T, K = x.shape
    na_blocks = wa.shape[1] // tn
    nb_blocks = wb.shape[1] // tn
    return pl.pallas_call(
        functools.partial(_linear_kernel, na_blocks=na_blocks),
        grid=(T // tm, na_blocks + nb_blocks),
        in_specs=[
            pl.BlockSpec((tm, K), lambda i, j: (i, 0)),
            pl.BlockSpec((K, tn), lambda i, j: (0, jnp.minimum(j, na_blocks - 1))),
            pl.BlockSpec((K, tn), lambda i, j: (0, jnp.maximum(j - na_blocks, 0))),
            pl.BlockSpec((1, tn), lambda i, j: (0, j)),
        ],
        out_specs=pl.BlockSpec((tm, tn), lambda i, j: (i, j)),
        out_shape=jax.ShapeDtypeStruct((T, (na_blocks + nb_blocks) * tn), BF16),
        compiler_params=_cparams(("arbitrary", "arbitrary")),
        name="in_proj",
    )(x, wa, wb, b)


def _conformer_kernel(a_ref, g_ref, dww_ref, dwb_ref, ng_ref, nb_ref, ow_ref, ob_ref, y_ref,
                      zbuf, cbuf, *, taps, halo, rb, cb):
    s = pl.program_id(1)
    ts, C = a_ref.shape

    @pl.when(s == 0)
    def _():
        zbuf[0:halo, :] = jnp.zeros((halo, C), F32)

    @pl.when(s != 0)
    def _():
        zbuf[0:halo, :] = zbuf[ts:ts + halo, :]

    zbuf[halo:halo + ts, :] = a_ref[...].astype(F32) * _sigmoid(g_ref[...].astype(F32))
    off = halo - (taps - 1)
    sub = SUBLANES_V7X
    for r0 in range(0, ts, rb):
        for c0 in range(0, C, cb):
            acc = jnp.broadcast_to(dwb_ref[:, c0:c0 + cb], (rb, cb))
            for r in range(sub):
                wl = rb + (sub if r else 0)
                q = None
                for a in range((off + taps - 1) // sub + 1):
                    j = sub * a + r - off
                    if 0 <= j < taps:
                        assert r0 + sub * a + wl <= halo + ts
                        term = dww_ref[j:j + 1, c0:c0 + cb] * zbuf[r0 + sub * a:r0 + sub * a + wl, c0:c0 + cb]
                        q = term if q is None else q + term
                if q is not None:
                    acc = acc + q[r:r + rb, :]
            cbuf[r0:r0 + rb, c0:c0 + cb] = acc
    zn = _ln_rows(cbuf[...], ng_ref[...], nb_ref[...])
    act = zn * _sigmoid(zn)
    y_ref[...] = jnp.dot(act.astype(BF16), ow_ref[...], preferred_element_type=F32) + ob_ref[...]


def _conformer(xp, dww, dwb, ng, nb, ow, ob, B, S, ts):
    T = xp.shape[0]
    C = dww.shape[1]
    D = ow.shape[1]
    taps = dww.shape[0]
    halo = -(-(taps - 1) // SUBLANES_V7X) * SUBLANES_V7X
    nS = S // ts
    kern = functools.partial(_conformer_kernel, taps=taps, halo=halo, rb=min(64, ts), cb=min(256, C))
    const = lambda b, s: (0, 0)
    return pl.pallas_call(
        kern,
        grid=(B, nS),
        in_specs=[
            pl.BlockSpec((ts, C), lambda b, s: (b * nS + s, 0)),
            pl.BlockSpec((ts, C), lambda b, s: (b * nS + s, 1)),
            pl.BlockSpec((taps, C), const),
            pl.BlockSpec((1, C), const),
            pl.BlockSpec((1, C), const),
            pl.BlockSpec((1, C), const),
            pl.BlockSpec((C, D), const),
            pl.BlockSpec((1, D), const),
        ],
        out_specs=pl.BlockSpec((ts, D), lambda b, s: (b * nS + s, 0)),
        out_shape=jax.ShapeDtypeStruct((T, D), F32),
        scratch_shapes=[pltpu.VMEM((halo + ts, C), F32), pltpu.VMEM((ts, C), F32)],
        compiler_params=_cparams(("arbitrary", "arbitrary")),
        name="conformer",
    )(xp, xp, dww, dwb, ng, nb, ow, ob)


def _mlstm_kernel(q_ref, k_ref, v_ref, o_ref, x_ref, wif_ref, bif_ref, wq_ref, bq_ref, wk_ref, bk_ref,
                  ng_ref, ow_ref, y_ref, qbuf, kbuf, c_ref, n_ref, m_ref, hbuf, *, heads, taps, halo):
    s = pl.program_id(1)
    L, W = q_ref.shape
    dh = W // heads

    @pl.when(s == 0)
    def _():
        qbuf[0:halo, :] = jnp.zeros((halo, W), F32)
        kbuf[0:halo, :] = jnp.zeros((halo, W), F32)
        c_ref[...] = jnp.zeros(c_ref.shape, F32)
        n_ref[...] = jnp.zeros(n_ref.shape, F32)
        m_ref[...] = jnp.zeros(m_ref.shape, F32)

    @pl.when(s != 0)
    def _():
        qbuf[0:halo, :] = qbuf[L:L + halo, :]
        kbuf[0:halo, :] = kbuf[L:L + halo, :]

    qbuf[halo:halo + L, :] = q_ref[...].astype(F32)
    kbuf[halo:halo + L, :] = k_ref[...].astype(F32)
    off = halo - (taps - 1)

    def short_conv(buf, w_ref, b_ref):
        acc = b_ref[...] + w_ref[0:1, :] * buf[off:off + L, :]
        for j in range(1, taps):
            acc = acc + w_ref[j:j + 1, :] * buf[off + j:off + j + L, :]
        return acc * _sigmoid(acc)

    qc = short_conv(qbuf, wq_ref, bq_ref)
    kc = short_conv(kbuf, wk_ref, bk_ref) * (dh ** -0.5)

    gates = jnp.dot(x_ref[...], wif_ref[...], preferred_element_type=F32) + bif_ref[...]
    logf = _log_sigmoid(gates)
    row = lax.broadcasted_iota(jnp.int32, (L, L), 0)
    col = lax.broadcasted_iota(jnp.int32, (L, L), 1)
    causal = row >= col
    bcum = jnp.dot(causal.astype(F32), logf, preferred_element_type=F32, precision=lax.Precision.HIGHEST)
    gates_t = gates.T
    bcum_t = bcum.T

    for h in range(heads):
        sl = slice(h * dh, (h + 1) * dh)
        qf = qc[:, sl]
        kf = kc[:, sl]
        qh = qf.astype(BF16)
        kh = kf.astype(BF16)
        vb = v_ref[:, sl]
        vf = vb.astype(F32)
        ig_c = gates[:, h:h + 1]
        ig_r = gates_t[h:h + 1, :]
        b_c = bcum[:, heads + h:heads + h + 1]
        b_r = bcum_t[heads + h:heads + h + 1, :]
        m_prev = m_ref[h:h + 1, 0:1]
        dmat = jnp.where(causal, b_c - b_r + ig_r, -jnp.inf)
        inter = b_c + m_prev
        m_row = jnp.maximum(inter, jnp.max(dmat, axis=1, keepdims=True))
        w_intra = jnp.exp(dmat - m_row)
        w_inter = jnp.exp(inter - m_row)
        sc = lax.dot_general(qh, kh, (((1,), (1,)), ((), ())), preferred_element_type=F32) * w_intra
        cmat = c_ref[h]
        num = w_inter * jnp.dot(qh, cmat.astype(BF16), preferred_element_type=F32) + jnp.dot(
            sc.astype(BF16), vb, preferred_element_type=F32)
        nvec = n_ref[h:h + 1, :]
        den = w_inter * jnp.sum(qf * nvec, axis=1, keepdims=True) + jnp.sum(sc, axis=1, keepdims=True)
        hh = num / jnp.maximum(jnp.abs(den), jnp.exp(-m_row))
        b_last = b_c[L - 1:L, :]
        lws_c = b_last - b_c + ig_c
        lws_r = b_last - b_r + ig_r
        m_new = jnp.maximum(b_last + m_prev, jnp.max(lws_r, axis=1, keepdims=True))
        ws_c = jnp.exp(lws_c - m_new)
        decay = jnp.exp(b_last + m_prev - m_new)
        kv = lax.dot_general(kh, (vf * ws_c).astype(BF16), (((0,), (0,)), ((), ())), preferred_element_type=F32)
        c_ref[h] = decay * cmat + kv
        n_ref[h:h + 1, :] = decay * nvec + jnp.sum(kf * ws_c, axis=0, keepdims=True)
        m_ref[h:h + 1, :] = jnp.broadcast_to(m_new, (1, m_ref.shape[1]))
        hn = _ln_rows(hh, ng_ref[:, sl], None)
        hbuf[:, sl] = hn * _sigmoid(o_ref[:, sl].astype(F32))
    y_ref[...] = jnp.dot(hbuf[...].astype(BF16), ow_ref[...], preferred_element_type=F32)


def _mlstm(xp, xb, wif, bif, wq, bq, wk, bk, ng, ow, B, S, L, heads, col_q):
    T = xp.shape[0]
    D = xb.shape[1]
    W = wq.shape[1]
    taps = wq.shape[0]
    dh = W // heads
    halo = SUBLANES_V7X
    nS = S // L
    kern = functools.partial(_mlstm_kernel, heads=heads, taps=taps, halo=halo)
    const = lambda b, s: (0, 0)

    def colspec(c):
        return pl.BlockSpec((L, W), lambda b, s: (b * nS + s, c))

    return pl.pallas_call(
        kern,
        grid=(B, nS),
        in_specs=[
            colspec(col_q), colspec(col_q + 1), colspec(col_q + 2), colspec(col_q + 3),
            pl.BlockSpec((L, D), lambda b, s: (b * nS + s, 0)),
            pl.BlockSpec(wif.shape, const),
            pl.BlockSpec(bif.shape, const),
            pl.BlockSpec((taps, W), const),
            pl.BlockSpec((1, W), const),
            pl.BlockSpec((taps, W), const),
            pl.BlockSpec((1, W), const),
            pl.BlockSpec((1, W), const),
            pl.BlockSpec((W, D), const),
        ],
        out_specs=pl.BlockSpec((L, D), lambda b, s: (b * nS + s, 0)),
        out_shape=jax.ShapeDtypeStruct((T, D), F32),
        scratch_shapes=[
            pltpu.VMEM((halo + L, W), F32),
            pltpu.VMEM((halo + L, W), F32),
            pltpu.VMEM((heads, dh, dh), F32),
            pltpu.VMEM((heads, dh), F32),
            pltpu.VMEM((heads, LANES_V7X), F32),
            pltpu.VMEM((L, W), F32),
        ],
        compiler_params=_cparams(("arbitrary", "arbitrary")),
        name="mlstm",
    )(xp, xp, xp, xp, xb, wif, bif, wq, bq, wk, bk, ng, ow)


def _merge_kernel(ga_ref, gb_ref, yc_ref, ym_ref, x_ref, wo_ref, lg_ref, lb_ref, rwt_ref, rb_ref,
                  x1_ref, idx_ref, gate_ref, rank_ref, cnt_ref, carry, *, alpha):
    i = pl.program_id(0)
    tm = x_ref.shape[0]
    E = rwt_ref.shape[0]

    @pl.when(i == 0)
    def _():
        carry[...] = jnp.zeros(carry.shape, F32)

    mix = _sigmoid(ga_ref[...].astype(F32)) * yc_ref[...] + _sigmoid(gb_ref[...].astype(F32)) * ym_ref[...]
    r = alpha * x_ref[...] + jnp.dot(mix.astype(BF16), wo_ref[...], preferred_element_type=F32)
    x1 = _ln_rows(r, lg_ref[...], lb_ref[...])
    _rows_to_tiles(x1_ref, x1)
    logits = lax.dot_general(rwt_ref[...], x1.astype(BF16), (((1,), (1,)), ((), ())),
                             preferred_element_type=F32) + rb_ref[:, 0:1]
    eidx = lax.broadcasted_iota(jnp.int32, (E, tm), 0).astype(F32)
    work = logits
    tops, hots = [], []
    for k in range(TOP_K):
        mk = jnp.max(work, axis=0, keepdims=True)
        ik = jnp.min(jnp.where(work == mk, eidx, float(E)), axis=0, keepdims=True)
        hot = eidx == ik
        work = jnp.where(hot, -jnp.inf, work)
        tops.append(mk)
        hots.append(hot)
        idx_ref[k:k + 1, :] = ik.astype(jnp.int32)
    exps = [jnp.exp(t - tops[0]) for t in tops]
    denom = exps[0]
    for e in exps[1:]:
        denom = denom + e
    for k in range(TOP_K):
        gate_ref[k:k + 1, :] = exps[k] / denom
    sel = hots[0].astype(F32)
    for hot in hots[1:]:
        sel = sel + hot.astype(F32)
    trow = lax.broadcasted_iota(jnp.int32, (tm, tm), 0)
    tcol = lax.broadcasted_iota(jnp.int32, (tm, tm), 1)
    before = (trow < tcol).astype(BF16)
    cum = jnp.dot(sel.astype(BF16), before, preferred_element_type=F32) + carry[:, 0:1]
    for k in range(TOP_K):
        rank_ref[k:k + 1, :] = jnp.sum(jnp.where(hots[k], cum, 0.0), axis=0, keepdims=True).astype(jnp.int32)
    total = carry[...] + jnp.sum(sel, axis=1, keepdims=True)
    carry[...] = total
    cnt_ref[...] = total


def _merge(xp, yc, ym, x, wo, lg, lb, rwt, rb, tm, col_ga, alpha):
    T, D = x.shape
    E = rwt.shape[0]
    kern = functools.partial(_merge_kernel, alpha=alpha)
    const = lambda i: (0, 0)
    row = lambda i: (i, 0)
    krow = pl.BlockSpec((TOP_K, tm), lambda i: (0, i))
    return pl.pallas_call(
        kern,
        grid=(T // tm,),
        in_specs=[
            pl.BlockSpec((tm, D), lambda i: (i, col_ga)),
            pl.BlockSpec((tm, D), lambda i: (i, col_ga + 1)),
            pl.BlockSpec((tm, D), row),
            pl.BlockSpec((tm, D), row),
            pl.BlockSpec((tm, D), row),
            pl.BlockSpec((D, D), const),
            pl.BlockSpec((1, D), const),
            pl.BlockSpec((1, D), const),
            pl.BlockSpec((E, D), const),
            pl.BlockSpec((E, LANES_V7X), const),
        ],
        out_specs=[
            pl.BlockSpec((tm * D // LANES_V7X, LANES_V7X), row),
            krow, krow, krow,
            pl.BlockSpec((E, LANES_V7X), const),
        ],
        out_shape=[
            jax.ShapeDtypeStruct((T * D // LANES_V7X, LANES_V7X), F32),
            jax.ShapeDtypeStruct((TOP_K, T), jnp.int32),
            jax.ShapeDtypeStruct((TOP_K, T), F32),
            jax.ShapeDtypeStruct((TOP_K, T), jnp.int32),
            jax.ShapeDtypeStruct((E, LANES_V7X), F32),
        ],
        scratch_shapes=[pltpu.VMEM((E, LANES_V7X), F32)],
        compiler_params=_cparams(("arbitrary",)),
        name="merge_router",
    )(xp, xp, yc, ym, x, wo, lg, lb, rwt, rb)


def _plan_kernel(cnt_ref, pstart_ref, blk_ref, meta_ref, *, blk):
    E = cnt_ref.shape[0]
    nbp = blk_ref.shape[1]
    cnt = cnt_ref[:, 0:1]
    padded = jnp.floor((cnt + (blk - 1)) / blk) * blk
    er = lax.broadcasted_iota(jnp.int32, (E, E), 0)
    ec = lax.broadcasted_iota(jnp.int32, (E, E), 1)
    padded_row = jnp.sum(jnp.where(er == ec, padded, 0.0), axis=0, keepdims=True)
    pstart = jnp.sum(jnp.where(ec < er, padded_row, 0.0), axis=1, keepdims=True)
    pend = pstart + padded
    pstart_ref[...] = jnp.broadcast_to(pstart, pstart_ref.shape)
    pend_row = jnp.sum(jnp.where(er <= ec, padded, 0.0), axis=0, keepdims=True)
    n_used = jnp.sum(padded_row, axis=1, keepdims=True) / blk
    meta_ref[...] = jnp.broadcast_to(n_used, meta_ref.shape).astype(jnp.int32)
    meta_ref[0:1, 0:E] = pend_row.astype(jnp.int32)
    first_row = lax.broadcasted_iota(jnp.int32, (E, nbp), 1).astype(F32) * blk
    inside = jnp.logical_and(pstart <= first_row, first_row < pend)
    eid = lax.broadcasted_iota(jnp.int32, (E, nbp), 0).astype(F32)
    block_e = jnp.sum(jnp.where(inside, eid, 0.0), axis=0, keepdims=True)
    valid = jnp.sum(jnp.where(inside, 1.0, 0.0), axis=0, keepdims=True)
    e_last = jnp.max(jnp.where(padded > 0.0, eid[:, 0:1], 0.0), axis=0, keepdims=True)
    block_e = jnp.where(valid > 0.0, block_e, e_last)
    blk_ref[...] = jnp.broadcast_to(block_e.astype(jnp.int32), blk_ref.shape)


def _plan(counts, blk, n_blocks):
    E = counts.shape[0]
    nbp = -(-n_blocks // LANES_V7X) * LANES_V7X
    return pl.pallas_call(
        functools.partial(_plan_kernel, blk=blk),
        out_shape=[
            jax.ShapeDtypeStruct((E, LANES_V7X), F32),
            jax.ShapeDtypeStruct((SUBLANES_V7X, nbp), jnp.int32),
            jax.ShapeDtypeStruct((SUBLANES_V7X, LANES_V7X), jnp.int32),
        ],
        name="route_plan",
    )(counts)


def _dest_kernel(idx_ref, rank_ref, pstart_ref, dest_ref):
    E = pstart_ref.shape[0]
    tm = idx_ref.shape[1]
    eidx = lax.broadcasted_iota(jnp.int32, (E, tm), 0)
    pstart = pstart_ref[:, 0:1]
    for k in range(TOP_K):
        base = jnp.sum(jnp.where(eidx == idx_ref[k:k + 1, :], pstart, 0.0), axis=0, keepdims=True)
        dest_ref[k:k + 1, :] = base.astype(jnp.int32) + rank_ref[k:k + 1, :]


def _dest(idx, rank, pstart, tm):
    T = idx.shape[1]
    E = pstart.shape[0]
    krow = pl.BlockSpec((TOP_K, tm), lambda i: (0, i))
    return pl.pallas_call(
        _dest_kernel,
        grid=(T // tm,),
        in_specs=[krow, krow, pl.BlockSpec((E, LANES_V7X), lambda i: (0, 0))],
        out_specs=krow,
        out_shape=jax.ShapeDtypeStruct((TOP_K, T), jnp.int32),
        compiler_params=_cparams(("arbitrary",)),
        name="route_dest",
    )(idx, rank, pstart)


def _dispatch_kernel(pend_ref, dest_hbm, x_ref, xs_hbm, dsm, zbuf, isem, rsem, zsem, *, td, T, blk, n_blocks):
    i = pl.program_id(0)
    E = pend_ref.shape[0] - 1
    r = x_ref.shape[0] // td
    idx_copies = [pltpu.make_async_copy(dest_hbm.at[pl.ds(k * T + i * td, td)], dsm.at[k], isem)
                  for k in range(TOP_K)]
    for c in idx_copies:
        c.start()

    @pl.when(i == 0)
    def _():
        zbuf[...] = jnp.zeros(zbuf.shape, zbuf.dtype)
        n_used = pend_ref[E]

        def clear(start):
            return pltpu.make_async_copy(zbuf, xs_hbm.at[pl.ds(pl.multiple_of(start * r, blk * r), blk * r)], zsem)

        def clears(fn):
            for e in range(E):
                @pl.when(pend_ref[e] >= blk)
                def _():
                    fn(clear(pend_ref[e] - blk))

                @pl.when(n_used + e < n_blocks)
                def _():
                    fn(clear((n_used + e) * blk))

        clears(lambda c: c.start())
        clears(lambda c: c.wait())

    for c in idx_copies:
        c.wait()

    def issue(t, c):
        for k in range(TOP_K):
            pltpu.make_async_copy(x_ref.at[pl.ds(pl.multiple_of(t * r, r), r)],
                                  xs_hbm.at[pl.ds(pl.multiple_of(dsm[k, t] * r, r), r)], rsem).start()
        return c

    lax.fori_loop(0, td, issue, 0, unroll=8)
    for k in range(TOP_K):
        pltpu.make_async_copy(x_ref, xs_hbm.at[pl.ds(0, td * r)], rsem).wait()


def _dispatch(pend, dest_flat, x1t, T, n_rows, td, blk):
    r = x1t.shape[0] // T
    return pl.pallas_call(
        functools.partial(_dispatch_kernel, td=td, T=T, blk=blk, n_blocks=n_rows // blk),
        grid_spec=pltpu.PrefetchScalarGridSpec(
            num_scalar_prefetch=1,
            grid=(T // td,),
            in_specs=[
                pl.BlockSpec(memory_space=pl.ANY),
                pl.BlockSpec((td * r, LANES_V7X), lambda i, pend: (i, 0)),
            ],
            out_specs=pl.BlockSpec(memory_space=pl.ANY),
            scratch_shapes=[
                pltpu.SMEM((TOP_K, td), jnp.int32),
                pltpu.VMEM((blk * r, LANES_V7X), F32),
                pltpu.SemaphoreType.DMA,
                pltpu.SemaphoreType.DMA,
                pltpu.SemaphoreType.DMA,
            ],
        ),
        out_shape=jax.ShapeDtypeStruct((n_rows * r, LANES_V7X), F32),
        compiler_params=_cparams(("arbitrary",)),
        name="moe_dispatch",
    )(pend, dest_flat, x1t)


def _expert_kernel(be_ref, nu_ref, xs_ref, w1_ref, b1_ref, w2_ref, b2_ref, y_ref, w1b, w2b):
    i = pl.program_id(0)
    F = w2_ref.shape[1]

    @pl.when(jnp.logical_or(i == 0, be_ref[i] != be_ref[jnp.maximum(i - 1, 0)]))
    def _():
        w1b[...] = w1_ref[0].astype(BF16)
        w2b[...] = w2_ref[0].astype(BF16)

    @pl.when(i < nu_ref[0])
    def _():
        xs = _rows_from_tiles(xs_ref, xs_ref.shape[0] * LANES_V7X // w1_ref.shape[1])
        h = jnp.dot(xs.astype(BF16), w1b[...], preferred_element_type=F32) + b1_ref[0]
        g = jnp.minimum(h[:, :F], SWIGLU_LIMIT)
        lin = jnp.clip(h[:, F:], -SWIGLU_LIMIT, SWIGLU_LIMIT)
        act = g * _sigmoid(SWIGLU_ALPHA * g) * (lin + 1.0)
        _rows_to_tiles(y_ref, jnp.dot(act.astype(BF16), w2b[...], preferred_element_type=F32) + b2_ref[0])

    @pl.when(i >= nu_ref[0])
    def _():
        y_ref[...] = jnp.zeros(y_ref.shape, F32)


def _experts(block_e, n_used, xs, w1, b1, w2, b2, blk):
    E, D, F2 = w1.shape
    F = w2.shape[1]
    r = D // LANES_V7X
    NR = xs.shape[0] // r
    wmap = lambda i, be, nu: (be[i], 0, 0)
    return pl.pallas_call(
        _expert_kernel,
        grid_spec=pltpu.PrefetchScalarGridSpec(
            num_scalar_prefetch=2,
            grid=(NR // blk,),
            in_specs=[
                pl.BlockSpec((blk * r, LANES_V7X), lambda i, be, nu: (jnp.minimum(i, nu[0] - 1), 0)),
                pl.BlockSpec((1, D, F2), wmap),
                pl.BlockSpec((1, 1, F2), wmap),
                pl.BlockSpec((1, F, D), wmap),
                pl.BlockSpec((1, 1, D), wmap),
            ],
            out_specs=pl.BlockSpec((blk * r, LANES_V7X), lambda i, be, nu: (i, 0)),
            scratch_shapes=[pltpu.VMEM((D, F2), BF16), pltpu.VMEM((F, D), BF16)],
        ),
        out_shape=jax.ShapeDtypeStruct((NR * r, LANES_V7X), F32),
        compiler_params=_cparams(("arbitrary",)),
        name="moe_experts",
    )(block_e, n_used, xs, w1, b1, w2, b2)


def _combine_kernel(dest_hbm, ys_hbm, x1_ref, gate_ref, lg_ref, lb_ref, o_ref, ob_ref, dsm, gbuf, isem, rsem,
                    *, tc, sub, T, alpha):
    i = pl.program_id(0)
    for k in range(TOP_K):
        pltpu.make_async_copy(dest_hbm.at[pl.ds(k * T + i * tc, tc)], dsm.at[k], isem).start()
    for k in range(TOP_K):
        pltpu.make_async_copy(dest_hbm.at[pl.ds(k * T + i * tc, tc)], dsm.at[k], isem).wait()

    r = x1_ref.shape[0] // tc
    n_sub = tc // sub

    def gather(j):
        slot = j % 2

        def issue(t, c):
            for k in range(TOP_K):
                pltpu.make_async_copy(ys_hbm.at[pl.ds(pl.multiple_of(dsm[k, j * sub + t] * r, r), r)],
                                      gbuf.at[slot, k, pl.ds(pl.multiple_of(t * r, r), r)], rsem.at[slot]).start()
            return c

        lax.fori_loop(0, sub, issue, 0, unroll=8)

    gather(0)
    for j in range(n_sub):
        slot = j % 2
        if j + 1 < n_sub:
            gather(j + 1)
        for k in range(TOP_K):
            pltpu.make_async_copy(ys_hbm.at[pl.ds(0, sub * r)], gbuf.at[slot, k], rsem.at[slot]).wait()
        rows = slice(j * sub, (j + 1) * sub)
        gate = gate_ref[rows, :]
        y = gate[:, 0:1] * _rows_from_tiles(gbuf.at[slot, 0], sub)
        for k in range(1, TOP_K):
            y = y + gate[:, k:k + 1] * _rows_from_tiles(gbuf.at[slot, k], sub)
        x1 = _rows_from_tiles(x1_ref.at[pl.ds(j * sub * r, sub * r)], sub)
        out = _ln_rows(alpha * x1 + y, lg_ref[...], lb_ref[...])
        o_ref[rows, :] = out
        ob_ref[rows, :] = out.astype(BF16)


def _combine(dest_flat, ys, x1t, gate_t, lg, lb, tc, sub, alpha):
    T = gate_t.shape[0]
    D = lg.shape[1]
    r = D // LANES_V7X
    const = lambda i: (0, 0)
    return pl.pallas_call(
        functools.partial(_combine_kernel, tc=tc, sub=sub, T=T, alpha=alpha),
        grid=(T // tc,),
        in_specs=[
            pl.BlockSpec(memory_space=pl.ANY),
            pl.BlockSpec(memory_space=pl.ANY),
            pl.BlockSpec((tc * r, LANES_V7X), lambda i: (i, 0)),
            pl.BlockSpec((tc, TOP_K), lambda i: (i, 0)),
            pl.BlockSpec((1, D), const),
            pl.BlockSpec((1, D), const),
        ],
        out_specs=[pl.BlockSpec((tc, D), lambda i: (i, 0)), pl.BlockSpec((tc, D), lambda i: (i, 0))],
        out_shape=[jax.ShapeDtypeStruct((T, D), F32), jax.ShapeDtypeStruct((T, D), BF16)],
        scratch_shapes=[
            pltpu.SMEM((TOP_K, tc), jnp.int32),
            pltpu.VMEM((2, TOP_K, sub * r, LANES_V7X), F32),
            pltpu.SemaphoreType.DMA,
            pltpu.SemaphoreType.DMA((2,)),
        ],
        compiler_params=_cparams(("arbitrary",)),
        name="moe_combine",
    )(dest_flat, ys, x1t, gate_t, lg, lb)


def _tile(n, pref):
    t = min(n, pref)
    assert n % t == 0, (n, pref)
    return t


def kernel(x, w_in, b_in, conv_dw_w, conv_dw_b, conv_norm_g, conv_norm_b, conv_out_w, conv_out_b, qk_conv_w, qk_conv_b, mlstm_norm_g, mlstm_out_w, w_out, ln1_g, ln1_b, router_w, router_b, moe_w1, moe_b1, moe_w2, moe_b2, ln2_g, ln2_b):
    B, S, D = x.shape
    depth, _, n_in = w_in.shape
    C = conv_dw_w.shape[2]
    W = mlstm_out_w.shape[1]
    E = router_w.shape[2]
    heads = (n_in - 2 * C - 4 * W - 2 * D) // 2
    assert C == D and W == D and n_in == 2 * C + 4 * W + 2 * heads + 2 * D
    T = B * S
    alpha = float((2 * depth) ** 0.25)
    off_i = 2 * C + 4 * W

    tm_proj = _tile(T, 1024)
    ts_conv = _tile(S, 256)
    l_chunk = _tile(S, 256)
    tm_merge = _tile(T, 512)
    td = _tile(T, 1024)
    sub = _tile(td, 256)
    blk = _tile(T * TOP_K, 512)
    n_blocks = T * TOP_K // blk + E
    n_rows = n_blocks * blk

    w_a = w_in[:, :, :off_i].astype(BF16)
    w_b = w_in[:, :, off_i + 2 * heads:].astype(BF16)
    b_main = jnp.concatenate([b_in[:, :off_i], b_in[:, off_i + 2 * heads:]], axis=1)[:, None, :]
    w_if = jnp.pad(w_in[:, :, off_i:off_i + 2 * heads], ((0, 0), (0, 0), (0, LANES_V7X - 2 * heads))).astype(BF16)
    b_if = jnp.pad(b_in[:, off_i:off_i + 2 * heads], ((0, 0), (0, LANES_V7X - 2 * heads)))[:, None, :]
    conv_ow = conv_out_w.astype(BF16)
    ml_ow = mlstm_out_w.astype(BF16)
    wo = w_out.astype(BF16)
    rwt = jnp.swapaxes(router_w, 1, 2).astype(BF16)
    rb = jnp.broadcast_to(router_b[:, :, None], (depth, E, LANES_V7X))

    xf = x.reshape(T, D)
    xb = xf.astype(BF16)
    for l in range(depth):
        xp = _linear(xb, w_a[l], w_b[l], b_main[l], tm_proj, _tile(2 * D, 1024))
        yc = _conformer(xp, conv_dw_w[l], conv_dw_b[l][None], conv_norm_g[l][None], conv_norm_b[l][None],
                        conv_ow[l], conv_out_b[l][None], B, S, ts_conv)
        ym = _mlstm(xp, xb, w_if[l], b_if[l], qk_conv_w[l][:, :W], qk_conv_b[l][None, :W],
                    qk_conv_w[l][:, W:], qk_conv_b[l][None, W:], mlstm_norm_g[l][None], ml_ow[l],
                    B, S, l_chunk, heads, 2 * C // W)
        x1t, idx, gate, rank, counts = _merge(xp, yc, ym, xf, wo[l], ln1_g[l][None], ln1_b[l][None], rwt[l], rb[l],
                                             tm_merge, (2 * C + 4 * W) // D, alpha)
        pstart, blocks, meta = _plan(counts, blk, n_blocks)
        dest = _dest(idx, rank, pstart, _tile(T, 2048)).reshape(TOP_K * T)
        xs = _dispatch(meta[0, :E + 1], dest, x1t, T, n_rows, td, blk)
        ys = _experts(blocks[0, :n_blocks], meta[1, :1], xs, moe_w1[l], moe_b1[l][:, None, :], moe_w2[l],
                      moe_b2[l][:, None, :], blk)
        xf, xb = _combine(dest, ys, x1t, gate.T, ln2_g[l][None], ln2_b[l][None], td, sub, alpha)
    return xf.reshape(B, S, D)
```

```python
import functools

import jax
import jax.numpy as jnp
from jax import lax
from jax.experimental import pallas as pl
from jax.experimental.pallas import tpu as pltpu

TOP_K = 4
SWIGLU_LIMIT = 7.0
SWIGLU_ALPHA = 1.702
LN_EPS = 1e-5
LANES_V7X = 128
SUBLANES_V7X = 8
VMEM_LIMIT_V7X = 56 * 1024 * 1024

F32 = jnp.float32
BF16 = jnp.bfloat16


def _cparams(sem):
    return pltpu.CompilerParams(dimension_semantics=sem, vmem_limit_bytes=VMEM_LIMIT_V7X)


def _ln_rows(x, g, b):
    mu = jnp.mean(x, axis=-1, keepdims=True)
    xc = x - mu
    var = jnp.mean(xc * xc, axis=-1, keepdims=True)
    y = xc * lax.rsqrt(var + LN_EPS) * g
    return y if b is None else y + b


def _sigmoid(x):
    return 1.0 / (1.0 + jnp.exp(-x))


def _log_sigmoid(x):
    return jnp.minimum(x, 0.0) - jnp.log(1.0 + jnp.exp(-jnp.abs(x)))


def _rows_from_tiles(ref, n):
    r = ref.shape[0] // n
    return jnp.concatenate([ref[pl.ds(s, n, stride=r), :] for s in range(r)], axis=1)


def _rows_to_tiles(ref, val):
    n = val.shape[0]
    r = ref.shape[0] // n
    for s in range(r):
        ref[pl.ds(s, n, stride=r), :] = val[:, s * LANES_V7X:(s + 1) * LANES_V7X]


def _linear_kernel(x_ref, wa_ref, wb_ref, b_ref, o_ref, *, na_blocks):
    j = pl.program_id(1)

    @pl.when(j < na_blocks)
    def _():
        o_ref[...] = (jnp.dot(x_ref[...], wa_ref[...], preferred_element_type=F32) + b_ref[...]).astype(o_ref.dtype)

    @pl.when(j >= na_blocks)
    def _():
        o_ref[...] = (jnp.dot(x_ref[...], wb_ref[...], preferred_element_type=F32) + b_ref[...]).astype(o_ref.dtype)


def _linear(x, wa, wb, b, tm, tn):
    T, K = x.shape
    na_blocks = wa.shape[1] // tn
    nb_blocks = wb.shape[1] // tn
    return pl.pallas_call(
        functools.partial(_linear_kernel, na_blocks=na_blocks),
        grid=(T // tm, na_blocks + nb_blocks),
        in_specs=[
            pl.BlockSpec((tm, K), lambda i, j: (i, 0)),
            pl.BlockSpec((K, tn), lambda i, j: (0, jnp.minimum(j, na_blocks - 1))),
            pl.BlockSpec((K, tn), lambda i, j: (0, jnp.maximum(j - na_blocks, 0))),
            pl.BlockSpec((1, tn), lambda i, j: (0, j)),
        ],
        out_specs=pl.BlockSpec((tm, tn), lambda i, j: (i, j)),
        out_shape=jax.ShapeDtypeStruct((T, (na_blocks + nb_blocks) * tn), BF16),
        compiler_params=_cparams(("arbitrary", "arbitrary")),
        name="in_proj",
    )(x, wa, wb, b)


def _conformer_kernel(a_ref, g_ref, dww_ref, dwb_ref, ng_ref, nb_ref, ow_ref, ob_ref, y_ref,
                      zbuf, cbuf, *, taps, halo, rb, cb):
    s = pl.program_id(1)
    ts, C = a_ref.shape

    @pl.when(s == 0)
    def _():
        zbuf[0:halo, :] = jnp.zeros((halo, C), F32)

    @pl.when(s != 0)
    def _():
        zbuf[0:halo, :] = zbuf[ts:ts + halo, :]

    zbuf[halo:halo + ts, :] = a_ref[...].astype(F32) * _sigmoid(g_ref[...].astype(F32))
    off = halo - (taps - 1)
    sub = SUBLANES_V7X
    for r0 in range(0, ts, rb):
        for c0 in range(0, C, cb):
            acc = jnp.broadcast_to(dwb_ref[:, c0:c0 + cb], (rb, cb))
            for r in range(sub):
                wl = rb + (sub if r else 0)
                q = None
                for a in range((off + taps - 1) // sub + 1):
                    j = sub * a + r - off
                    if 0 <= j < taps:
                        assert r0 + sub * a + wl <= halo + ts
                        term = dww_ref[j:j + 1, c0:c0 + cb] * zbuf[r0 + sub * a:r0 + sub * a + wl, c0:c0 + cb]
                        q = term if q is None else q + term
                if q is not None:
                    acc = acc + q[r:r + rb, :]
            cbuf[r0:r0 + rb, c0:c0 + cb] = acc
    zn = _ln_rows(cbuf[...], ng_ref[...], nb_ref[...])
    act = zn * _sigmoid(zn)
    y_ref[...] = jnp.dot(act.astype(BF16), ow_ref[...], preferred_element_type=F32) + ob_ref[...]


def _conformer(xp, dww, dwb, ng, nb, ow, ob, B, S, ts):
    T = xp.shape[0]
    C = dww.shape[1]
    D = ow.shape[1]
    taps = dww.shape[0]
    halo = -(-(taps - 1) // SUBLANES_V7X) * SUBLANES_V7X
    nS = S // ts
    kern = functools.partial(_conformer_kernel, taps=taps, halo=halo, rb=min(64, ts), cb=min(256, C))
    const = lambda b, s: (0, 0)
    return pl.pallas_call(
        kern,
        grid=(B, nS),
        in_specs=[
            pl.BlockSpec((ts, C), lambda b, s: (b * nS + s, 0)),
            pl.BlockSpec((ts, C), lambda b, s: (b * nS + s, 1)),
            pl.BlockSpec((taps, C), const),
            pl.BlockSpec((1, C), const),
            pl.BlockSpec((1, C), const),
            pl.BlockSpec((1, C), const),
            pl.BlockSpec((C, D), const),
            pl.BlockSpec((1, D), const),
        ],
        out_specs=pl.BlockSpec((ts, D), lambda b, s: (b * nS + s, 0)),
        out_shape=jax.ShapeDtypeStruct((T, D), F32),
        scratch_shapes=[pltpu.VMEM((halo + ts, C), F32), pltpu.VMEM((ts, C), F32)],
        compiler_params=_cparams(("arbitrary", "arbitrary")),
        name="conformer",
    )(xp, xp, dww, dwb, ng, nb, ow, ob)


def _mlstm_kernel(q_ref, k_ref, v_ref, o_ref, x_ref, wif_ref, bif_ref, wq_ref, bq_ref, wk_ref, bk_ref,
                  ng_ref, ow_ref, y_ref, qbuf, kbuf, c_ref, n_ref, m_ref, hbuf, *, heads, taps, halo):
    s = pl.program_id(1)
    L, W = q_ref.shape
    dh = W // heads

    @pl.when(s == 0)
    def _():
        qbuf[0:halo, :] = jnp.zeros((halo, W), F32)
        kbuf[0:halo, :] = jnp.zeros((halo, W), F32)
        c_ref[...] = jnp.zeros(c_ref.shape, F32)
        n_ref[...] = jnp.zeros(n_ref.shape, F32)
        m_ref[...] = jnp.zeros(m_ref.shape, F32)

    @pl.when(s != 0)
    def _():
        qbuf[0:halo, :] = qbuf[L:L + halo, :]
        kbuf[0:halo, :] = kbuf[L:L + halo, :]

    qbuf[halo:halo + L, :] = q_ref[...].astype(F32)
    kbuf[halo:halo + L, :] = k_ref[...].astype(F32)
    off = halo - (taps - 1)

    def short_conv(buf, w_ref, b_ref):
        acc = b_ref[...] + w_ref[0:1, :] * buf[off:off + L, :]
        for j in range(1, taps):
            acc = acc + w_ref[j:j + 1, :] * buf[off + j:off + j + L, :]
        return acc * _sigmoid(acc)

    qc = short_conv(qbuf, wq_ref, bq_ref)
    kc = short_conv(kbuf, wk_ref, bk_ref) * (dh ** -0.5)

    gates = jnp.dot(x_ref[...], wif_ref[...], preferred_element_type=F32) + bif_ref[...]
    logf = _log_sigmoid(gates)
    row = lax.broadcasted_iota(jnp.int32, (L, L), 0)
    col = lax.broadcasted_iota(jnp.int32, (L, L), 1)
    causal = row >= col
    bcum = jnp.dot(causal.astype(F32), logf, preferred_element_type=F32, precision=lax.Precision.HIGHEST)
    gates_t = gates.T
    bcum_t = bcum.T

    for h in range(heads):
        sl = slice(h * dh, (h + 1) * dh)
        qf = qc[:, sl]
        kf = kc[:, sl]
        qh = qf.astype(BF16)
        kh = kf.astype(BF16)
        vb = v_ref[:, sl]
        vf = vb.astype(F32)
        ig_c = gates[:, h:h + 1]
        ig_r = gates_t[h:h + 1, :]
        b_c = bcum[:, heads + h:heads + h + 1]
        b_r = bcum_t[heads + h:heads + h + 1, :]
        m_prev = m_ref[h:h + 1, 0:1]
        dmat = jnp.where(causal, b_c - b_r + ig_r, -jnp.inf)
        inter = b_c + m_prev
        m_row = jnp.maximum(inter, jnp.max(dmat, axis=1, keepdims=True))
        w_intra = jnp.exp(dmat - m_row)
        w_inter = jnp.exp(inter - m_row)
        sc = lax.dot_general(qh, kh, (((1,), (1,)), ((), ())), preferred_element_type=F32) * w_intra
        cmat = c_ref[h]
        num = w_inter * jnp.dot(qh, cmat.astype(BF16), preferred_element_type=F32) + jnp.dot(
            sc.astype(BF16), vb, preferred_element_type=F32)
        nvec = n_ref[h:h + 1, :]
        den = w_inter * jnp.sum(qf * nvec, axis=1, keepdims=True) + jnp.sum(sc, axis=1, keepdims=True)
        hh = num / jnp.maximum(jnp.abs(den), jnp.exp(-m_row))
        b_last = b_c[L - 1:L, :]
        lws_c = b_last - b_c + ig_c
        lws_r = b_last - b_r + ig_r
        m_new = jnp.maximum(b_last + m_prev, jnp.max(lws_r, axis=1, keepdims=True))
        ws_c = jnp.exp(lws_c - m_new)
        decay = jnp.exp(b_last + m_prev - m_new)
        kv = lax.dot_general(kh, (vf * ws_c).astype(BF16), (((0,), (0,)), ((), ())), preferred_element_type=F32)
        c_ref[h] = decay * cmat + kv
        n_ref[h:h + 1, :] = decay * nvec + jnp.sum(kf * ws_c, axis=0, keepdims=True)
        m_ref[h:h + 1, :] = jnp.broadcast_to(m_new, (1, m_ref.shape[1]))
        hn = _ln_rows(hh, ng_ref[:, sl], None)
        hbuf[:, sl] = hn * _sigmoid(o_ref[:, sl].astype(F32))
    y_ref[...] = jnp.dot(hbuf[...].astype(BF16), ow_ref[...], preferred_element_type=F32)


def _mlstm(xp, xb, wif, bif, wq, bq, wk, bk, ng, ow, B, S, L, heads, col_q):
    T = xp.shape[0]
    D = xb.shape[1]
    W = wq.shape[1]
    taps = wq.shape[0]
    dh = W // heads
    halo = SUBLANES_V7X
    nS = S // L
    kern = functools.partial(_mlstm_kernel, heads=heads, taps=taps, halo=halo)
    const = lambda b, s: (0, 0)

    def colspec(c):
        return pl.BlockSpec((L, W), lambda b, s: (b * nS + s, c))

    return pl.pallas_call(
        kern,
        grid=(B, nS),
        in_specs=[
            colspec(col_q), colspec(col_q + 1), colspec(col_q + 2), colspec(col_q + 3),
            pl.BlockSpec((L, D), lambda b, s: (b * nS + s, 0)),
            pl.BlockSpec(wif.shape, const),
            pl.BlockSpec(bif.shape, const),
            pl.BlockSpec((taps, W), const),
            pl.BlockSpec((1, W), const),
            pl.BlockSpec((taps, W), const),
            pl.BlockSpec((1, W), const),
            pl.BlockSpec((1, W), const),
            pl.BlockSpec((W, D), const),
        ],
        out_specs=pl.BlockSpec((L, D), lambda b, s: (b * nS + s, 0)),
        out_shape=jax.ShapeDtypeStruct((T, D), F32),
        scratch_shapes=[
            pltpu.VMEM((halo + L, W), F32),
            pltpu.VMEM((halo + L, W), F32),
            pltpu.VMEM((heads, dh, dh), F32),
            pltpu.VMEM((heads, dh), F32),
            pltpu.VMEM((heads, LANES_V7X), F32),
            pltpu.VMEM((L, W), F32),
        ],
        compiler_params=_cparams(("arbitrary", "arbitrary")),
        name="mlstm",
    )(xp, xp, xp, xp, xb, wif, bif, wq, bq, wk, bk, ng, ow)


def _merge_kernel(ga_ref, gb_ref, yc_ref, ym_ref, x_ref, wo_ref, lg_ref, lb_ref, rwt_ref, rb_ref,
                  x1_ref, gate_ref, ploc_ref, tmeta_ref, cnt_ref, carry, *, alpha):
    i = pl.program_id(0)
    tm = x_ref.shape[0]
    E = rwt_ref.shape[0]
    rpt = x1_ref.shape[0] // tm

    @pl.when(i == 0)
    def _():
        carry[...] = jnp.zeros(carry.shape, F32)

    mix = _sigmoid(ga_ref[...].astype(F32)) * yc_ref[...] + _sigmoid(gb_ref[...].astype(F32)) * ym_ref[...]
    r = alpha * x_ref[...] + jnp.dot(mix.astype(BF16), wo_ref[...], preferred_element_type=F32)
    x1 = _ln_rows(r, lg_ref[...], lb_ref[...])
    _rows_to_tiles(x1_ref, x1)
    logits = lax.dot_general(rwt_ref[...], x1.astype(BF16), (((1,), (1,)), ((), ())),
                             preferred_element_type=F32) + rb_ref[:, 0:1]
    eidx = lax.broadcasted_iota(jnp.int32, (E, tm), 0).astype(F32)
    work = logits
    tops, hots = [], []
    for k in range(TOP_K):
        mk = jnp.max(work, axis=0, keepdims=True)
        ik = jnp.min(jnp.where(work == mk, eidx, float(E)), axis=0, keepdims=True)
        hot = eidx == ik
        work = jnp.where(hot, -jnp.inf, work)
        tops.append(mk)
        hots.append(hot)
    exps = [jnp.exp(t - tops[0]) for t in tops]
    denom = exps[0]
    for e in exps[1:]:
        denom = denom + e
    for k in range(TOP_K):
        gate_ref[k:k + 1, :] = exps[k] / denom
    sel = hots[0].astype(F32)
    for hot in hots[1:]:
        sel = sel + hot.astype(F32)
    trow = lax.broadcasted_iota(jnp.int32, (tm, tm), 0)
    tcol = lax.broadcasted_iota(jnp.int32, (tm, tm), 1)
    before = (trow < tcol).astype(BF16)
    earlier = jnp.dot(sel.astype(BF16), before, preferred_element_type=F32)
    cnt_tile = jnp.sum(sel, axis=1, keepdims=True)
    er = lax.broadcasted_iota(jnp.int32, (E, E), 0)
    ec = lax.broadcasted_iota(jnp.int32, (E, E), 1)
    cnt_row = jnp.sum(jnp.where(er == ec, cnt_tile, 0.0), axis=0, keepdims=True)
    pos = jnp.sum(jnp.where(ec < er, cnt_row, 0.0), axis=1, keepdims=True) + earlier
    for k in range(TOP_K):
        ploc_ref[k:k + 1, :] = (jnp.sum(jnp.where(hots[k], pos, 0.0), axis=0, keepdims=True) * rpt).astype(jnp.int32)
    seen_row = jnp.sum(jnp.where(er == ec, carry[:, 0:1], 0.0), axis=0, keepdims=True)
    tmeta_ref[...] = jnp.zeros(tmeta_ref.shape, jnp.int32)
    tmeta_ref[0, 0:1, 0:E] = cnt_row.astype(jnp.int32)
    tmeta_ref[0, 1:2, 0:E] = seen_row.astype(jnp.int32)
    total = carry[...] + cnt_tile
    carry[...] = total
    cnt_ref[...] = total


def _merge(xp, yc, ym, x, wo, lg, lb, rwt, rb, tm, col_ga, alpha):
    T, D = x.shape
    E = rwt.shape[0]
    kern = functools.partial(_merge_kernel, alpha=alpha)
    const = lambda i: (0, 0)
    row = lambda i: (i, 0)
    krow = pl.BlockSpec((TOP_K, tm), lambda i: (0, i))
    return pl.pallas_call(
        kern,
        grid=(T // tm,),
        in_specs=[
            pl.BlockSpec((tm, D), lambda i: (i, col_ga)),
            pl.BlockSpec((tm, D), lambda i: (i, col_ga + 1)),
            pl.BlockSpec((tm, D), row),
            pl.BlockSpec((tm, D), row),
            pl.BlockSpec((tm, D), row),
            pl.BlockSpec((D, D), const),
            pl.BlockSpec((1, D), const),
            pl.BlockSpec((1, D), const),
            pl.BlockSpec((E, D), const),
            pl.BlockSpec((E, LANES_V7X), const),
        ],
        out_specs=[
            pl.BlockSpec((tm * D // LANES_V7X, LANES_V7X), row),
            krow, krow,
            pl.BlockSpec((1, SUBLANES_V7X, LANES_V7X), lambda i: (i, 0, 0)),
            pl.BlockSpec((E, LANES_V7X), const),
        ],
        out_shape=[
            jax.ShapeDtypeStruct((T * D // LANES_V7X, LANES_V7X), F32),
            jax.ShapeDtypeStruct((TOP_K, T), F32),
            jax.ShapeDtypeStruct((TOP_K, T), jnp.int32),
            jax.ShapeDtypeStruct((T // tm, SUBLANES_V7X, LANES_V7X), jnp.int32),
            jax.ShapeDtypeStruct((E, LANES_V7X), F32),
        ],
        scratch_shapes=[pltpu.VMEM((E, LANES_V7X), F32)],
        compiler_params=_cparams(("arbitrary",)),
        name="merge_router",
    )(xp, xp, yc, ym, x, wo, lg, lb, rwt, rb)


def _plan_kernel(cnt_ref, blk_ref, meta_ref, *, blk):
    E = cnt_ref.shape[0]
    nbp = blk_ref.shape[1]
    cnt = cnt_ref[:, 0:1]
    padded = jnp.floor((cnt + (blk - 1)) / blk) * blk
    er = lax.broadcasted_iota(jnp.int32, (E, E), 0)
    ec = lax.broadcasted_iota(jnp.int32, (E, E), 1)
    padded_row = jnp.sum(jnp.where(er == ec, padded, 0.0), axis=0, keepdims=True)
    pstart = jnp.sum(jnp.where(ec < er, padded_row, 0.0), axis=1, keepdims=True)
    pend = pstart + padded
    pend_row = jnp.sum(jnp.where(er <= ec, padded, 0.0), axis=0, keepdims=True)
    n_used = jnp.sum(padded_row, axis=1, keepdims=True) / blk
    meta_ref[...] = jnp.broadcast_to(n_used, meta_ref.shape).astype(jnp.int32)
    meta_ref[0:1, 0:E] = pend_row.astype(jnp.int32)
    meta_ref[1:2, 0:E] = (pend_row - padded_row).astype(jnp.int32)
    first_row = lax.broadcasted_iota(jnp.int32, (E, nbp), 1).astype(F32) * blk
    inside = jnp.logical_and(pstart <= first_row, first_row < pend)
    eid = lax.broadcasted_iota(jnp.int32, (E, nbp), 0).astype(F32)
    block_e = jnp.sum(jnp.where(inside, eid, 0.0), axis=0, keepdims=True)
    valid = jnp.sum(jnp.where(inside, 1.0, 0.0), axis=0, keepdims=True)
    e_last = jnp.max(jnp.where(padded > 0.0, eid[:, 0:1], 0.0), axis=0, keepdims=True)
    block_e = jnp.where(valid > 0.0, block_e, e_last)
    blk_ref[...] = jnp.broadcast_to(block_e.astype(jnp.int32), blk_ref.shape)


def _plan(counts, blk, n_blocks):
    E = counts.shape[0]
    nbp = -(-n_blocks // LANES_V7X) * LANES_V7X
    return pl.pallas_call(
        functools.partial(_plan_kernel, blk=blk),
        out_shape=[
            jax.ShapeDtypeStruct((SUBLANES_V7X, nbp), jnp.int32),
            jax.ShapeDtypeStruct((SUBLANES_V7X, LANES_V7X), jnp.int32),
        ],
        name="route_plan",
    )(counts)


def _run_copies(n, local_row, global_row, buf, hbm, sem, rpt, max_rows, to_hbm):
    for b in range(max_rows.bit_length()):
        size = 1 << b
        done = n & ~(2 * size - 1)

        @pl.when((n & size) != 0)
        def _():
            loc = buf.at[pl.ds(pl.multiple_of((local_row + done) * rpt, rpt), size * rpt)]
            glob = hbm.at[pl.ds(pl.multiple_of((global_row + done) * rpt, rpt), size * rpt)]
            src, dst = (loc, glob) if to_hbm else (glob, loc)
            pltpu.make_async_copy(src, dst, sem).start()


def _tile_runs(j, pstart_ref, msm, buf, hbm, sem, rpt, sub, to_hbm):
    def runs(e, off):
        n = msm[j, 0, e]
        _run_copies(n, off, pstart_ref[e] + msm[j, 1, e], buf, hbm, sem, rpt, sub, to_hbm)
        return off + n

    lax.fori_loop(0, pstart_ref.shape[0], runs, 0)


def _dispatch_kernel(pend_ref, pstart_ref, ploc_hbm, tmeta_hbm, x_ref, xs_hbm, psm, msm, sbuf, zbuf, isem, rsem, zsem,
                     *, td, sub, T, blk, n_blocks):
    i = pl.program_id(0)
    E = pend_ref.shape[0] - 1
    r = x_ref.shape[0] // td
    n_sub = td // sub
    idx_copies = [pltpu.make_async_copy(ploc_hbm.at[pl.ds(k * T + i * td, td)], psm.at[pl.ds(k * td, td)], isem)
                  for k in range(TOP_K)]
    idx_copies.append(pltpu.make_async_copy(tmeta_hbm.at[pl.ds(i * n_sub, n_sub)], msm, isem))
    for c in idx_copies:
        c.start()

    @pl.when(i == 0)
    def _():
        zbuf[...] = jnp.zeros(zbuf.shape, zbuf.dtype)
        n_used = pend_ref[E]

        def clear(start):
            return pltpu.make_async_copy(zbuf, xs_hbm.at[pl.ds(pl.multiple_of(start * r, blk * r), blk * r)], zsem)

        def clears(fn):
            for e in range(E):
                @pl.when(pend_ref[e] >= blk)
                def _():
                    fn(clear(pend_ref[e] - blk))

                @pl.when(n_used + e < n_blocks)
                def _():
                    fn(clear((n_used + e) * blk))

        clears(lambda c: c.start())
        clears(lambda c: c.wait())

    for c in idx_copies:
        c.wait()

    def drain(slot):
        pltpu.make_async_copy(sbuf.at[slot], xs_hbm.at[pl.ds(0, TOP_K * sub * r)], rsem.at[slot]).wait()

    for j in range(n_sub):
        slot = j % 2

        def place(t, c):
            row = x_ref[pl.ds(pl.multiple_of((j * sub + t) * r, r), r), :]
            for k in range(TOP_K):
                sbuf[slot, pl.ds(pl.multiple_of(psm[k * td + j * sub + t], r), r), :] = row
            return c

        lax.fori_loop(0, sub, place, 0, unroll=8)
        _tile_runs(j, pstart_ref, msm, sbuf.at[slot], xs_hbm, rsem.at[slot], r, sub, True)
        if j >= 1:
            drain(1 - slot)
    drain((n_sub - 1) % 2)


def _dispatch(pend, pstart, ploc_flat, tmeta, x1t, T, n_rows, td, sub, blk):
    r = x1t.shape[0] // T
    return pl.pallas_call(
        functools.partial(_dispatch_kernel, td=td, sub=sub, T=T, blk=blk, n_blocks=n_rows // blk),
        grid_spec=pltpu.PrefetchScalarGridSpec(
            num_scalar_prefetch=2,
            grid=(T // td,),
            in_specs=[
                pl.BlockSpec(memory_space=pl.ANY),
                pl.BlockSpec(memory_space=pl.ANY),
                pl.BlockSpec((td * r, LANES_V7X), lambda i, pend, pstart: (i, 0)),
            ],
            out_specs=pl.BlockSpec(memory_space=pl.ANY),
            scratch_shapes=[
                pltpu.SMEM((TOP_K * td,), jnp.int32),
                pltpu.SMEM((td // sub, SUBLANES_V7X, LANES_V7X), jnp.int32),
                pltpu.VMEM((2, TOP_K * sub * r, LANES_V7X), F32),
                pltpu.VMEM((blk * r, LANES_V7X), F32),
                pltpu.SemaphoreType.DMA,
                pltpu.SemaphoreType.DMA((2,)),
                pltpu.SemaphoreType.DMA,
            ],
        ),
        out_shape=jax.ShapeDtypeStruct((n_rows * r, LANES_V7X), F32),
        compiler_params=_cparams(("arbitrary",)),
        name="moe_dispatch",
    )(pend, pstart, ploc_flat, tmeta, x1t)


def _expert_kernel(be_ref, nu_ref, xs_ref, w1_ref, b1_ref, w2_ref, b2_ref, y_ref, w1b, w2b):
    i = pl.program_id(0)
    F = w2_ref.shape[1]

    @pl.when(jnp.logical_or(i == 0, be_ref[i] != be_ref[jnp.maximum(i - 1, 0)]))
    def _():
        w1b[...] = w1_ref[0].astype(BF16)
        w2b[...] = w2_ref[0].astype(BF16)

    @pl.when(i < nu_ref[0])
    def _():
        xs = _rows_from_tiles(xs_ref, xs_ref.shape[0] * LANES_V7X // w1_ref.shape[1])
        h = jnp.dot(xs.astype(BF16), w1b[...], preferred_element_type=F32) + b1_ref[0]
        g = jnp.minimum(h[:, :F], SWIGLU_LIMIT)
        lin = jnp.clip(h[:, F:], -SWIGLU_LIMIT, SWIGLU_LIMIT)
        act = g * _sigmoid(SWIGLU_ALPHA * g) * (lin + 1.0)
        _rows_to_tiles(y_ref, jnp.dot(act.astype(BF16), w2b[...], preferred_element_type=F32) + b2_ref[0])

    @pl.when(i >= nu_ref[0])
    def _():
        y_ref[...] = jnp.zeros(y_ref.shape, F32)


def _experts(block_e, n_used, xs, w1, b1, w2, b2, blk):
    E, D, F2 = w1.shape
    F = w2.shape[1]
    r = D // LANES_V7X
    NR = xs.shape[0] // r
    wmap = lambda i, be, nu: (be[i], 0, 0)
    return pl.pallas_call(
        _expert_kernel,
        grid_spec=pltpu.PrefetchScalarGridSpec(
            num_scalar_prefetch=2,
            grid=(NR // blk,),
            in_specs=[
                pl.BlockSpec((blk * r, LANES_V7X), lambda i, be, nu: (jnp.minimum(i, nu[0] - 1), 0)),
                pl.BlockSpec((1, D, F2), wmap),
                pl.BlockSpec((1, 1, F2), wmap),
                pl.BlockSpec((1, F, D), wmap),
                pl.BlockSpec((1, 1, D), wmap),
            ],
            out_specs=pl.BlockSpec((blk * r, LANES_V7X), lambda i, be, nu: (i, 0)),
            scratch_shapes=[pltpu.VMEM((D, F2), BF16), pltpu.VMEM((F, D), BF16)],
        ),
        out_shape=jax.ShapeDtypeStruct((NR * r, LANES_V7X), F32),
        compiler_params=_cparams(("arbitrary",)),
        name="moe_experts",
    )(block_e, n_used, xs, w1, b1, w2, b2)


def _combine_kernel(pstart_ref, ploc_hbm, tmeta_hbm, ys_hbm, x1_ref, gate_ref, lg_ref, lb_ref, o_ref, ob_ref,
                    psm, msm, sbuf, gbuf, isem, rsem, *, tc, sub, T, alpha):
    i = pl.program_id(0)
    r = x1_ref.shape[0] // tc
    n_sub = tc // sub
    idx_copies = [pltpu.make_async_copy(ploc_hbm.at[pl.ds(k * T + i * tc, tc)], psm.at[pl.ds(k * tc, tc)], isem)
                  for k in range(TOP_K)]
    idx_copies.append(pltpu.make_async_copy(tmeta_hbm.at[pl.ds(i * n_sub, n_sub)], msm, isem))
    for c in idx_copies:
        c.start()
    for c in idx_copies:
        c.wait()

    def fetch(j):
        _tile_runs(j, pstart_ref, msm, sbuf.at[j % 2], ys_hbm, rsem.at[j % 2], r, sub, False)

    fetch(0)
    for j in range(n_sub):
        slot = j % 2
        if j + 1 < n_sub:
            fetch(j + 1)
        pltpu.make_async_copy(ys_hbm.at[pl.ds(0, TOP_K * sub * r)], sbuf.at[slot], rsem.at[slot]).wait()

        def pick(t, c):
            for k in range(TOP_K):
                gbuf[k, pl.ds(pl.multiple_of(t * r, r), r), :] = sbuf[
                    slot, pl.ds(pl.multiple_of(psm[k * tc + j * sub + t], r), r), :]
            return c

        lax.fori_loop(0, sub, pick, 0, unroll=8)
        rows = slice(j * sub, (j + 1) * sub)
        gate = gate_ref[rows, :]
        y = gate[:, 0:1] * _rows_from_tiles(gbuf.at[0], sub)
        for k in range(1, TOP_K):
            y = y + gate[:, k:k + 1] * _rows_from_tiles(gbuf.at[k], sub)
        x1 = _rows_from_tiles(x1_ref.at[pl.ds(j * sub * r, sub * r)], sub)
        out = _ln_rows(alpha * x1 + y, lg_ref[...], lb_ref[...])
        o_ref[rows, :] = out
        ob_ref[rows, :] = out.astype(BF16)


def _combine(pstart, ploc_flat, tmeta, ys, x1t, gate_t, lg, lb, tc, sub, alpha):
    T = gate_t.shape[0]
    D = lg.shape[1]
    r = D // LANES_V7X
    const = lambda i, ps: (0, 0)
    row = lambda i, ps: (i, 0)
    return pl.pallas_call(
        functools.partial(_combine_kernel, tc=tc, sub=sub, T=T, alpha=alpha),
        grid_spec=pltpu.PrefetchScalarGridSpec(
            num_scalar_prefetch=1,
            grid=(T // tc,),
            in_specs=[
                pl.BlockSpec(memory_space=pl.ANY),
                pl.BlockSpec(memory_space=pl.ANY),
                pl.BlockSpec(memory_space=pl.ANY),
                pl.BlockSpec((tc * r, LANES_V7X), row),
                pl.BlockSpec((tc, TOP_K), row),
                pl.BlockSpec((1, D), const),
                pl.BlockSpec((1, D), const),
            ],
            out_specs=[pl.BlockSpec((tc, D), row), pl.BlockSpec((tc, D), row)],
            scratch_shapes=[
                pltpu.SMEM((TOP_K * tc,), jnp.int32),
                pltpu.SMEM((tc // sub, SUBLANES_V7X, LANES_V7X), jnp.int32),
                pltpu.VMEM((2, TOP_K * sub * r, LANES_V7X), F32),
                pltpu.VMEM((TOP_K, sub * r, LANES_V7X), F32),
                pltpu.SemaphoreType.DMA,
                pltpu.SemaphoreType.DMA((2,)),
            ],
        ),
        out_shape=[jax.ShapeDtypeStruct((T, D), F32), jax.ShapeDtypeStruct((T, D), BF16)],
        compiler_params=_cparams(("arbitrary",)),
        name="moe_combine",
    )(pstart, ploc_flat, tmeta, ys, x1t, gate_t, lg, lb)


def _tile(n, pref):
    t = min(n, pref)
    assert n % t == 0, (n, pref)
    return t


def kernel(x, w_in, b_in, conv_dw_w, conv_dw_b, conv_norm_g, conv_norm_b, conv_out_w, conv_out_b, qk_conv_w, qk_conv_b, mlstm_norm_g, mlstm_out_w, w_out, ln1_g, ln1_b, router_w, router_b, moe_w1, moe_b1, moe_w2, moe_b2, ln2_g, ln2_b):
    B, S, D = x.shape
    depth, _, n_in = w_in.shape
    C = conv_dw_w.shape[2]
    W = mlstm_out_w.shape[1]
    E = router_w.shape[2]
    heads = (n_in - 2 * C - 4 * W - 2 * D) // 2
    assert C == D and W == D and n_in == 2 * C + 4 * W + 2 * heads + 2 * D
    T = B * S
    alpha = float((2 * depth) ** 0.25)
    off_i = 2 * C + 4 * W

    tm_proj = _tile(T, 1024)
    ts_conv = _tile(S, 256)
    l_chunk = _tile(S, 256)
    td = _tile(T, 1024)
    sub = _tile(td, 256)
    blk = _tile(T * TOP_K, 512)
    n_blocks = T * TOP_K // blk + E
    n_rows = n_blocks * blk

    w_a = w_in[:, :, :off_i].astype(BF16)
    w_b = w_in[:, :, off_i + 2 * heads:].astype(BF16)
    b_main = jnp.concatenate([b_in[:, :off_i], b_in[:, off_i + 2 * heads:]], axis=1)[:, None, :]
    w_if = jnp.pad(w_in[:, :, off_i:off_i + 2 * heads], ((0, 0), (0, 0), (0, LANES_V7X - 2 * heads))).astype(BF16)
    b_if = jnp.pad(b_in[:, off_i:off_i + 2 * heads], ((0, 0), (0, LANES_V7X - 2 * heads)))[:, None, :]
    conv_ow = conv_out_w.astype(BF16)
    ml_ow = mlstm_out_w.astype(BF16)
    wo = w_out.astype(BF16)
    rwt = jnp.swapaxes(router_w, 1, 2).astype(BF16)
    rb = jnp.broadcast_to(router_b[:, :, None], (depth, E, LANES_V7X))

    xf = x.reshape(T, D)
    xb = xf.astype(BF16)
    for l in range(depth):
        xp = _linear(xb, w_a[l], w_b[l], b_main[l], tm_proj, _tile(2 * D, 1024))
        yc = _conformer(xp, conv_dw_w[l], conv_dw_b[l][None], conv_norm_g[l][None], conv_norm_b[l][None],
                        conv_ow[l], conv_out_b[l][None], B, S, ts_conv)
        ym = _mlstm(xp, xb, w_if[l], b_if[l], qk_conv_w[l][:, :W], qk_conv_b[l][None, :W],
                    qk_conv_w[l][:, W:], qk_conv_b[l][None, W:], mlstm_norm_g[l][None], ml_ow[l],
                    B, S, l_chunk, heads, 2 * C // W)
        x1t, gate, ploc, tmeta, counts = _merge(xp, yc, ym, xf, wo[l], ln1_g[l][None], ln1_b[l][None], rwt[l],
                                                rb[l], sub, (2 * C + 4 * W) // D, alpha)
        blocks, meta = _plan(counts, blk, n_blocks)
        ploc = ploc.reshape(TOP_K * T)
        xs = _dispatch(meta[0, :E + 1], meta[1, :E], ploc, tmeta, x1t, T, n_rows, td, sub, blk)
        ys = _experts(blocks[0, :n_blocks], meta[2, :1], xs, moe_w1[l], moe_b1[l][:, None, :], moe_w2[l],
                      moe_b2[l][:, None, :], blk)
        xf, xb = _combine(meta[1, :E], ploc, tmeta, ys, x1t, gate.T, ln2_g[l][None], ln2_b[l][None], td, sub, alpha)
    return xf.reshape(B, S, D)
```

```python
import functools

import jax
import jax.numpy as jnp
from jax import lax
from jax.experimental import pallas as pl
from jax.experimental.pallas import tpu as pltpu

TOP_K = 4
SWIGLU_LIMIT = 7.0
SWIGLU_ALPHA = 1.702
LN_EPS = 1e-5
LANES_V7X = 128
SUBLANES_V7X = 8
VMEM_LIMIT_V7X = 56 * 1024 * 1024

F32 = jnp.float32
BF16 = jnp.bfloat16


def _cparams(sem):
    return pltpu.CompilerParams(dimension_semantics=sem, vmem_limit_bytes=VMEM_LIMIT_V7X)


def _ln_rows(x, g, b):
    mu = jnp.mean(x, axis=-1, keepdims=True)
    xc = x - mu
    var = jnp.mean(xc * xc, axis=-1, keepdims=True)
    y = xc * lax.rsqrt(var + LN_EPS) * g
    return y if b is None else y + b


def _sigmoid(x):
    return 1.0 / (1.0 + jnp.exp(-x))


def _log_sigmoid(x):
    return jnp.minimum(x, 0.0) - jnp.log(1.0 + jnp.exp(-jnp.abs(x)))


def _rows_from_tiles(ref, n):
    r = ref.shape[0] // n
    return jnp.concatenate([ref[pl.ds(s, n, stride=r), :] for s in range(r)], axis=1)


def _rows_to_tiles(ref, val):
    n = val.shape[0]
    r = ref.shape[0] // n
    for s in range(r):
        ref[pl.ds(s, n, stride=r), :] = val[:, s * LANES_V7X:(s + 1) * LANES_V7X]


def _linear_kernel(x_ref, wlo_ref, whi_ref, b_ref, o_ref, wbf, *, n_aligned, skip, kc):
    j = pl.program_id(0)
    i = pl.program_id(1)
    K, tn = wbf.shape

    @pl.when(jnp.logical_and(i == 0, j < n_aligned))
    def _():
        wbf[...] = wlo_ref[0].astype(BF16)

    @pl.when(jnp.logical_and(i == 0, j >= n_aligned))
    def _():
        for r0 in range(0, K, kc):
            both = jnp.concatenate([wlo_ref[0, r0:r0 + kc, :], whi_ref[0, r0:r0 + kc, :]], axis=1)
            wbf[r0:r0 + kc, :] = pltpu.roll(both, 2 * tn - skip, axis=1)[:, :tn].astype(BF16)

    o_ref[...] = (jnp.dot(x_ref[...], wbf[...], preferred_element_type=F32) + b_ref[...]).astype(o_ref.dtype)


def _linear(x, w_all, layer, b, n_aligned, n_blocks, skip, tm, tn):
    T, K = x.shape
    last = -(-w_all.shape[2] // tn) - 1
    return pl.pallas_call(
        functools.partial(_linear_kernel, n_aligned=n_aligned, skip=skip, kc=min(K, 256)),
        grid=(n_blocks, T // tm),
        in_specs=[
            pl.BlockSpec((tm, K), lambda j, i: (i, 0)),
            pl.BlockSpec((1, K, tn), lambda j, i: (layer, 0, j)),
            pl.BlockSpec((1, K, tn), lambda j, i: (layer, 0, jnp.minimum(j + 1, last))),
            pl.BlockSpec((1, tn), lambda j, i: (0, j)),
        ],
        out_specs=pl.BlockSpec((tm, tn), lambda j, i: (i, j)),
        out_shape=jax.ShapeDtypeStruct((T, n_blocks * tn), BF16),
        scratch_shapes=[pltpu.VMEM((K, tn), BF16)],
        compiler_params=_cparams(("arbitrary", "arbitrary")),
        name="in_proj",
    )(x, w_all, w_all, b)


def _conformer_kernel(a_ref, g_ref, dww_ref, dwb_ref, ng_ref, nb_ref, ow_ref, ob_ref, y_ref,
                      zbuf, cbuf, *, taps, halo, rb, cb):
    s = pl.program_id(1)
    ts, C = a_ref.shape

    @pl.when(s == 0)
    def _():
        zbuf[0:halo, :] = jnp.zeros((halo, C), F32)

    @pl.when(s != 0)
    def _():
        zbuf[0:halo, :] = zbuf[ts:ts + halo, :]

    zbuf[halo:halo + ts, :] = a_ref[...].astype(F32) * _sigmoid(g_ref[...].astype(F32))
    off = halo - (taps - 1)
    sub = SUBLANES_V7X
    for r0 in range(0, ts, rb):
        for c0 in range(0, C, cb):
            acc = jnp.broadcast_to(dwb_ref[:, c0:c0 + cb], (rb, cb))
            for r in range(sub):
                wl = rb + (sub if r else 0)
                q = None
                for a in range((off + taps - 1) // sub + 1):
                    j = sub * a + r - off
                    if 0 <= j < taps:
                        assert r0 + sub * a + wl <= halo + ts
                        term = dww_ref[j:j + 1, c0:c0 + cb] * zbuf[r0 + sub * a:r0 + sub * a + wl, c0:c0 + cb]
                        q = term if q is None else q + term
                if q is not None:
                    acc = acc + q[r:r + rb, :]
            cbuf[r0:r0 + rb, c0:c0 + cb] = acc
    zn = _ln_rows(cbuf[...], ng_ref[...], nb_ref[...])
    act = zn * _sigmoid(zn)
    y_ref[...] = jnp.dot(act.astype(BF16), ow_ref[...], preferred_element_type=F32) + ob_ref[...]


def _conformer(xp, dww, dwb, ng, nb, ow, ob, B, S, ts):
    T = xp.shape[0]
    C = dww.shape[1]
    D = ow.shape[1]
    taps = dww.shape[0]
    halo = -(-(taps - 1) // SUBLANES_V7X) * SUBLANES_V7X
    nS = S // ts
    kern = functools.partial(_conformer_kernel, taps=taps, halo=halo, rb=min(64, ts), cb=min(256, C))
    const = lambda b, s: (0, 0)
    return pl.pallas_call(
        kern,
        grid=(B, nS),
        in_specs=[
            pl.BlockSpec((ts, C), lambda b, s: (b * nS + s, 0)),
            pl.BlockSpec((ts, C), lambda b, s: (b * nS + s, 1)),
            pl.BlockSpec((taps, C), const),
            pl.BlockSpec((1, C), const),
            pl.BlockSpec((1, C), const),
            pl.BlockSpec((1, C), const),
            pl.BlockSpec((C, D), const),
            pl.BlockSpec((1, D), const),
        ],
        out_specs=pl.BlockSpec((ts, D), lambda b, s: (b * nS + s, 0)),
        out_shape=jax.ShapeDtypeStruct((T, D), F32),
        scratch_shapes=[pltpu.VMEM((halo + ts, C), F32), pltpu.VMEM((ts, C), F32)],
        compiler_params=_cparams(("arbitrary", "arbitrary")),
        name="conformer",
    )(xp, xp, dww, dwb, ng, nb, ow, ob)


def _mlstm_kernel(q_ref, k_ref, v_ref, o_ref, x_ref, wif_ref, bif_ref, wq_ref, bq_ref, wk_ref, bk_ref,
                  ng_ref, ow_ref, y_ref, qbuf, kbuf, c_ref, n_ref, m_ref, hbuf, *, heads, taps, halo):
    s = pl.program_id(1)
    L, W = q_ref.shape
    dh = W // heads

    @pl.when(s == 0)
    def _():
        qbuf[0:halo, :] = jnp.zeros((halo, W), F32)
        kbuf[0:halo, :] = jnp.zeros((halo, W), F32)
        c_ref[...] = jnp.zeros(c_ref.shape, F32)
        n_ref[...] = jnp.zeros(n_ref.shape, F32)
        m_ref[...] = jnp.zeros(m_ref.shape, F32)

    @pl.when(s != 0)
    def _():
        qbuf[0:halo, :] = qbuf[L:L + halo, :]
        kbuf[0:halo, :] = kbuf[L:L + halo, :]

    qbuf[halo:halo + L, :] = q_ref[...].astype(F32)
    kbuf[halo:halo + L, :] = k_ref[...].astype(F32)
    off = halo - (taps - 1)

    def short_conv(buf, w_ref, b_ref):
        acc = b_ref[...] + w_ref[0:1, :] * buf[off:off + L, :]
        for j in range(1, taps):
            acc = acc + w_ref[j:j + 1, :] * buf[off + j:off + j + L, :]
        return acc * _sigmoid(acc)

    qc = short_conv(qbuf, wq_ref, bq_ref)
    kc = short_conv(kbuf, wk_ref, bk_ref) * (dh ** -0.5)

    gates = jnp.dot(x_ref[...], wif_ref[...], preferred_element_type=F32) + bif_ref[...]
    logf = _log_sigmoid(gates)
    row = lax.broadcasted_iota(jnp.int32, (L, L), 0)
    col = lax.broadcasted_iota(jnp.int32, (L, L), 1)
    causal = row >= col
    bcum = jnp.dot(causal.astype(F32), logf, preferred_element_type=F32, precision=lax.Precision.HIGHEST)
    gates_t = gates.T
    bcum_t = bcum.T

    for h in range(heads):
        sl = slice(h * dh, (h + 1) * dh)
        qf = qc[:, sl]
        kf = kc[:, sl]
        qh = qf.astype(BF16)
        kh = kf.astype(BF16)
        vb = v_ref[:, sl]
        vf = vb.astype(F32)
        ig_c = gates[:, h:h + 1]
        ig_r = gates_t[h:h + 1, :]
        b_c = bcum[:, heads + h:heads + h + 1]
        b_r = bcum_t[heads + h:heads + h + 1, :]
        m_prev = m_ref[h:h + 1, 0:1]
        dmat = jnp.where(causal, b_c - b_r + ig_r, -jnp.inf)
        inter = b_c + m_prev
        m_row = jnp.maximum(inter, jnp.max(dmat, axis=1, keepdims=True))
        w_intra = jnp.exp(dmat - m_row)
        w_inter = jnp.exp(inter - m_row)
        sc = lax.dot_general(qh, kh, (((1,), (1,)), ((), ())), preferred_element_type=F32) * w_intra
        cmat = c_ref[h]
        num = w_inter * jnp.dot(qh, cmat.astype(BF16), preferred_element_type=F32) + jnp.dot(
            sc.astype(BF16), vb, preferred_element_type=F32)
        nvec = n_ref[h:h + 1, :]
        den = w_inter * jnp.sum(qf * nvec, axis=1, keepdims=True) + jnp.sum(sc, axis=1, keepdims=True)
        hh = num / jnp.maximum(jnp.abs(den), jnp.exp(-m_row))
        b_last = b_c[L - 1:L, :]
        lws_c = b_last - b_c + ig_c
        lws_r = b_last - b_r + ig_r
        m_new = jnp.maximum(b_last + m_prev, jnp.max(lws_r, axis=1, keepdims=True))
        ws_c = jnp.exp(lws_c - m_new)
        decay = jnp.exp(b_last + m_prev - m_new)
        kv = lax.dot_general(kh, (vf * ws_c).astype(BF16), (((0,), (0,)), ((), ())), preferred_element_type=F32)
        c_ref[h] = decay * cmat + kv
        n_ref[h:h + 1, :] = decay * nvec + jnp.sum(kf * ws_c, axis=0, keepdims=True)
        m_ref[h:h + 1, :] = jnp.broadcast_to(m_new, (1, m_ref.shape[1]))
        hn = _ln_rows(hh, ng_ref[:, sl], None)
        hbuf[:, sl] = hn * _sigmoid(o_ref[:, sl].astype(F32))
    y_ref[...] = jnp.dot(hbuf[...].astype(BF16), ow_ref[...], preferred_element_type=F32)


def _mlstm(xp, xb, wif, bif, wq, bq, wk, bk, ng, ow, B, S, L, heads, col_q):
    T = xp.shape[0]
    D = xb.shape[1]
    W = wq.shape[1]
    taps = wq.shape[0]
    dh = W // heads
    halo = SUBLANES_V7X
    nS = S // L
    kern = functools.partial(_mlstm_kernel, heads=heads, taps=taps, halo=halo)
    const = lambda b, s: (0, 0)

    def colspec(c):
        return pl.BlockSpec((L, W), lambda b, s: (b * nS + s, c))

    return pl.pallas_call(
        kern,
        grid=(B, nS),
        in_specs=[
            colspec(col_q), colspec(col_q + 1), colspec(col_q + 2), colspec(col_q + 3),
            pl.BlockSpec((L, D), lambda b, s: (b * nS + s, 0)),
            pl.BlockSpec(wif.shape, const),
            pl.BlockSpec(bif.shape, const),
            pl.BlockSpec((taps, W), const),
            pl.BlockSpec((1, W), const),
            pl.BlockSpec((taps, W), const),
            pl.BlockSpec((1, W), const),
            pl.BlockSpec((1, W), const),
            pl.BlockSpec((W, D), const),
        ],
        out_specs=pl.BlockSpec((L, D), lambda b, s: (b * nS + s, 0)),
        out_shape=jax.ShapeDtypeStruct((T, D), F32),
        scratch_shapes=[
            pltpu.VMEM((halo + L, W), F32),
            pltpu.VMEM((halo + L, W), F32),
            pltpu.VMEM((heads, dh, dh), F32),
            pltpu.VMEM((heads, dh), F32),
            pltpu.VMEM((heads, LANES_V7X), F32),
            pltpu.VMEM((L, W), F32),
        ],
        compiler_params=_cparams(("arbitrary", "arbitrary")),
        name="mlstm",
    )(xp, xp, xp, xp, xb, wif, bif, wq, bq, wk, bk, ng, ow)


def _merge_kernel(ga_ref, gb_ref, yc_ref, ym_ref, x_ref, wo_ref, lg_ref, lb_ref, rwt_ref, rb_ref,
                  x1_ref, gate_ref, ploc_ref, tmeta_ref, cnt_ref, carry, *, alpha, sub):
    i = pl.program_id(0)
    tm = x_ref.shape[0]
    E = rwt_ref.shape[0]
    rpt = x1_ref.shape[0] // tm

    @pl.when(i == 0)
    def _():
        carry[...] = jnp.zeros(carry.shape, F32)

    mix = _sigmoid(ga_ref[...].astype(F32)) * yc_ref[...] + _sigmoid(gb_ref[...].astype(F32)) * ym_ref[...]
    r = alpha * x_ref[...] + jnp.dot(mix.astype(BF16), wo_ref[...], preferred_element_type=F32)
    x1 = _ln_rows(r, lg_ref[...], lb_ref[...])
    _rows_to_tiles(x1_ref, x1)
    logits = lax.dot_general(rwt_ref[...], x1.astype(BF16), (((1,), (1,)), ((), ())),
                             preferred_element_type=F32) + rb_ref[:, 0:1]
    eidx = lax.broadcasted_iota(jnp.int32, (E, tm), 0).astype(F32)
    work = logits
    tops, hots = [], []
    for k in range(TOP_K):
        mk = jnp.max(work, axis=0, keepdims=True)
        ik = jnp.min(jnp.where(work == mk, eidx, float(E)), axis=0, keepdims=True)
        hot = eidx == ik
        work = jnp.where(hot, -jnp.inf, work)
        tops.append(mk)
        hots.append(hot)
    exps = [jnp.exp(t - tops[0]) for t in tops]
    denom = exps[0]
    for e in exps[1:]:
        denom = denom + e
    for k in range(TOP_K):
        gate_ref[k:k + 1, :] = exps[k] / denom
    trow = lax.broadcasted_iota(jnp.int32, (sub, sub), 0)
    tcol = lax.broadcasted_iota(jnp.int32, (sub, sub), 1)
    before = (trow < tcol).astype(BF16)
    er = lax.broadcasted_iota(jnp.int32, (E, E), 0)
    ec = lax.broadcasted_iota(jnp.int32, (E, E), 1)
    tmeta_ref[...] = jnp.zeros(tmeta_ref.shape, jnp.int32)
    seen = carry[...]
    for u in range(tm // sub):
        lanes = slice(u * sub, (u + 1) * sub)
        tile_hots = [hot[:, lanes] for hot in hots]
        sel = tile_hots[0].astype(F32)
        for hot in tile_hots[1:]:
            sel = sel + hot.astype(F32)
        earlier = jnp.dot(sel.astype(BF16), before, preferred_element_type=F32)
        cnt_tile = jnp.sum(sel, axis=1, keepdims=True)
        cnt_row = jnp.sum(jnp.where(er == ec, cnt_tile, 0.0), axis=0, keepdims=True)
        pos = jnp.sum(jnp.where(ec < er, cnt_row, 0.0), axis=1, keepdims=True) + earlier
        for k in range(TOP_K):
            ploc_ref[k:k + 1, lanes] = (
                jnp.sum(jnp.where(tile_hots[k], pos, 0.0), axis=0, keepdims=True) * rpt).astype(jnp.int32)
        seen_row = jnp.sum(jnp.where(er == ec, seen[:, 0:1], 0.0), axis=0, keepdims=True)
        tmeta_ref[u, 0:1, 0:E] = cnt_row.astype(jnp.int32)
        tmeta_ref[u, 1:2, 0:E] = seen_row.astype(jnp.int32)
        seen = seen + cnt_tile
    carry[...] = seen
    cnt_ref[...] = seen


def _merge(xp, yc, ym, x, wo, lg, lb, rwt, rb, tm, sub, col_ga, alpha):
    T, D = x.shape
    E = rwt.shape[0]
    kern = functools.partial(_merge_kernel, alpha=alpha, sub=sub)
    const = lambda i: (0, 0)
    row = lambda i: (i, 0)
    krow = pl.BlockSpec((TOP_K, tm), lambda i: (0, i))
    return pl.pallas_call(
        kern,
        grid=(T // tm,),
        in_specs=[
            pl.BlockSpec((tm, D), lambda i: (i, col_ga)),
            pl.BlockSpec((tm, D), lambda i: (i, col_ga + 1)),
            pl.BlockSpec((tm, D), row),
            pl.BlockSpec((tm, D), row),
            pl.BlockSpec((tm, D), row),
            pl.BlockSpec((D, D), const),
            pl.BlockSpec((1, D), const),
            pl.BlockSpec((1, D), const),
            pl.BlockSpec((E, D), const),
            pl.BlockSpec((E, LANES_V7X), const),
        ],
        out_specs=[
            pl.BlockSpec((tm * D // LANES_V7X, LANES_V7X), row),
            krow, krow,
            pl.BlockSpec((tm // sub, SUBLANES_V7X, LANES_V7X), lambda i: (i, 0, 0)),
            pl.BlockSpec((E, LANES_V7X), const),
        ],
        out_shape=[
            jax.ShapeDtypeStruct((T * D // LANES_V7X, LANES_V7X), F32),
            jax.ShapeDtypeStruct((TOP_K, T), F32),
            jax.ShapeDtypeStruct((TOP_K, T), jnp.int32),
            jax.ShapeDtypeStruct((T // sub, SUBLANES_V7X, LANES_V7X), jnp.int32),
            jax.ShapeDtypeStruct((E, LANES_V7X), F32),
        ],
        scratch_shapes=[pltpu.VMEM((E, LANES_V7X), F32)],
        compiler_params=_cparams(("arbitrary",)),
        name="merge_router",
    )(xp, xp, yc, ym, x, wo, lg, lb, rwt, rb)


def _plan_kernel(cnt_ref, blk_ref, meta_ref, *, blk):
    E = cnt_ref.shape[0]
    nbp = blk_ref.shape[1]
    cnt = cnt_ref[:, 0:1]
    padded = jnp.floor((cnt + (blk - 1)) / blk) * blk
    er = lax.broadcasted_iota(jnp.int32, (E, E), 0)
    ec = lax.broadcasted_iota(jnp.int32, (E, E), 1)
    padded_row = jnp.sum(jnp.where(er == ec, padded, 0.0), axis=0, keepdims=True)
    pstart = jnp.sum(jnp.where(ec < er, padded_row, 0.0), axis=1, keepdims=True)
    pend = pstart + padded
    pend_row = jnp.sum(jnp.where(er <= ec, padded, 0.0), axis=0, keepdims=True)
    n_used = jnp.sum(padded_row, axis=1, keepdims=True) / blk
    meta_ref[...] = jnp.broadcast_to(n_used, meta_ref.shape).astype(jnp.int32)
    meta_ref[0:1, 0:E] = pend_row.astype(jnp.int32)
    meta_ref[1:2, 0:E] = (pend_row - padded_row).astype(jnp.int32)
    first_row = lax.broadcasted_iota(jnp.int32, (E, nbp), 1).astype(F32) * blk
    inside = jnp.logical_and(pstart <= first_row, first_row < pend)
    eid = lax.broadcasted_iota(jnp.int32, (E, nbp), 0).astype(F32)
    block_e = jnp.sum(jnp.where(inside, eid, 0.0), axis=0, keepdims=True)
    valid = jnp.sum(jnp.where(inside, 1.0, 0.0), axis=0, keepdims=True)
    e_last = jnp.max(jnp.where(padded > 0.0, eid[:, 0:1], 0.0), axis=0, keepdims=True)
    block_e = jnp.where(valid > 0.0, block_e, e_last)
    blk_ref[...] = jnp.broadcast_to(block_e.astype(jnp.int32), blk_ref.shape)


def _plan(counts, blk, n_blocks):
    E = counts.shape[0]
    nbp = -(-n_blocks // LANES_V7X) * LANES_V7X
    return pl.pallas_call(
        functools.partial(_plan_kernel, blk=blk),
        out_shape=[
            jax.ShapeDtypeStruct((SUBLANES_V7X, nbp), jnp.int32),
            jax.ShapeDtypeStruct((SUBLANES_V7X, LANES_V7X), jnp.int32),
        ],
        name="route_plan",
    )(counts)


def _run_copies(n, local_row, global_row, buf, hbm, sem, rpt, max_rows, to_hbm):
    for b in range(max_rows.bit_length()):
        size = 1 << b
        done = n & ~(2 * size - 1)

        @pl.when((n & size) != 0)
        def _():
            loc = buf.at[pl.ds(pl.multiple_of((local_row + done) * rpt, rpt), size * rpt)]
            glob = hbm.at[pl.ds(pl.multiple_of((global_row + done) * rpt, rpt), size * rpt)]
            src, dst = (loc, glob) if to_hbm else (glob, loc)
            pltpu.make_async_copy(src, dst, sem).start()


def _tile_runs(j, pstart_ref, msm, buf, hbm, sem, rpt, sub, to_hbm):
    def runs(e, off):
        n = msm[j, 0, e]
        _run_copies(n, off, pstart_ref[e] + msm[j, 1, e], buf, hbm, sem, rpt, sub, to_hbm)
        return off + n

    lax.fori_loop(0, pstart_ref.shape[0], runs, 0)


def _dispatch_kernel(pend_ref, pstart_ref, ploc_hbm, tmeta_hbm, x_ref, xs_hbm, psm, msm, sbuf, zbuf, isem, rsem, zsem,
                     *, td, sub, T, blk, n_blocks):
    i = pl.program_id(0)
    E = pend_ref.shape[0] - 1
    r = x_ref.shape[0] // td
    n_sub = td // sub
    idx_copies = [pltpu.make_async_copy(ploc_hbm.at[pl.ds(k * T + i * td, td)], psm.at[pl.ds(k * td, td)], isem)
                  for k in range(TOP_K)]
    idx_copies.append(pltpu.make_async_copy(tmeta_hbm.at[pl.ds(i * n_sub, n_sub)], msm, isem))
    for c in idx_copies:
        c.start()

    @pl.when(i == 0)
    def _():
        zbuf[...] = jnp.zeros(zbuf.shape, zbuf.dtype)
        n_used = pend_ref[E]

        def clear(start):
            return pltpu.make_async_copy(zbuf, xs_hbm.at[pl.ds(pl.multiple_of(start * r, blk * r), blk * r)], zsem)

        def clears(fn):
            for e in range(E):
                @pl.when(pend_ref[e] >= blk)
                def _():
                    fn(clear(pend_ref[e] - blk))

                @pl.when(n_used + e < n_blocks)
                def _():
                    fn(clear((n_used + e) * blk))

        clears(lambda c: c.start())
        clears(lambda c: c.wait())

    for c in idx_copies:
        c.wait()

    def drain(slot):
        pltpu.make_async_copy(sbuf.at[slot], xs_hbm.at[pl.ds(0, TOP_K * sub * r)], rsem.at[slot]).wait()

    for j in range(n_sub):
        slot = j % 2

        def place(t, c):
            row = x_ref[pl.ds(pl.multiple_of((j * sub + t) * r, r), r), :]
            for k in range(TOP_K):
                sbuf[slot, pl.ds(pl.multiple_of(psm[k * td + j * sub + t], r), r), :] = row
            return c

        lax.fori_loop(0, sub, place, 0, unroll=8)
        _tile_runs(j, pstart_ref, msm, sbuf.at[slot], xs_hbm, rsem.at[slot], r, sub, True)
        if j >= 1:
            drain(1 - slot)
    drain((n_sub - 1) % 2)


def _dispatch(pend, pstart, ploc_flat, tmeta, x1t, T, n_rows, td, sub, blk):
    r = x1t.shape[0] // T
    return pl.pallas_call(
        functools.partial(_dispatch_kernel, td=td, sub=sub, T=T, blk=blk, n_blocks=n_rows // blk),
        grid_spec=pltpu.PrefetchScalarGridSpec(
            num_scalar_prefetch=2,
            grid=(T // td,),
            in_specs=[
                pl.BlockSpec(memory_space=pl.ANY),
                pl.BlockSpec(memory_space=pl.ANY),
                pl.BlockSpec((td * r, LANES_V7X), lambda i, pend, pstart: (i, 0)),
            ],
            out_specs=pl.BlockSpec(memory_space=pl.ANY),
            scratch_shapes=[
                pltpu.SMEM((TOP_K * td,), jnp.int32),
                pltpu.SMEM((td // sub, SUBLANES_V7X, LANES_V7X), jnp.int32),
                pltpu.VMEM((2, TOP_K * sub * r, LANES_V7X), F32),
                pltpu.VMEM((blk * r, LANES_V7X), F32),
                pltpu.SemaphoreType.DMA,
                pltpu.SemaphoreType.DMA((2,)),
                pltpu.SemaphoreType.DMA,
            ],
        ),
        out_shape=jax.ShapeDtypeStruct((n_rows * r, LANES_V7X), F32),
        compiler_params=_cparams(("arbitrary",)),
        name="moe_dispatch",
    )(pend, pstart, ploc_flat, tmeta, x1t)


def _expert_kernel(be_ref, nu_ref, xs_ref, w1_ref, b1_ref, w2_ref, b2_ref, y_ref, w1b, w2b):
    i = pl.program_id(0)
    F = w2_ref.shape[1]

    @pl.when(jnp.logical_or(i == 0, be_ref[i] != be_ref[jnp.maximum(i - 1, 0)]))
    def _():
        w1b[...] = w1_ref[0].astype(BF16)
        w2b[...] = w2_ref[0].astype(BF16)

    @pl.when(i < nu_ref[0])
    def _():
        xs = _rows_from_tiles(xs_ref, xs_ref.shape[0] * LANES_V7X // w1_ref.shape[1])
        h = jnp.dot(xs.astype(BF16), w1b[...], preferred_element_type=F32) + b1_ref[0]
        g = jnp.minimum(h[:, :F], SWIGLU_LIMIT)
        lin = jnp.clip(h[:, F:], -SWIGLU_LIMIT, SWIGLU_LIMIT)
        act = g * _sigmoid(SWIGLU_ALPHA * g) * (lin + 1.0)
        _rows_to_tiles(y_ref, jnp.dot(act.astype(BF16), w2b[...], preferred_element_type=F32) + b2_ref[0])

    @pl.when(i >= nu_ref[0])
    def _():
        y_ref[...] = jnp.zeros(y_ref.shape, F32)


def _experts(block_e, n_used, xs, w1, b1, w2, b2, blk):
    E, D, F2 = w1.shape
    F = w2.shape[1]
    r = D // LANES_V7X
    NR = xs.shape[0] // r
    wmap = lambda i, be, nu: (be[i], 0, 0)
    return pl.pallas_call(
        _expert_kernel,
        grid_spec=pltpu.PrefetchScalarGridSpec(
            num_scalar_prefetch=2,
            grid=(NR // blk,),
            in_specs=[
                pl.BlockSpec((blk * r, LANES_V7X), lambda i, be, nu: (jnp.minimum(i, nu[0] - 1), 0)),
                pl.BlockSpec((1, D, F2), wmap),
                pl.BlockSpec((1, 1, F2), wmap),
                pl.BlockSpec((1, F, D), wmap),
                pl.BlockSpec((1, 1, D), wmap),
            ],
            out_specs=pl.BlockSpec((blk * r, LANES_V7X), lambda i, be, nu: (i, 0)),
            scratch_shapes=[pltpu.VMEM((D, F2), BF16), pltpu.VMEM((F, D), BF16)],
        ),
        out_shape=jax.ShapeDtypeStruct((NR * r, LANES_V7X), F32),
        compiler_params=_cparams(("arbitrary",)),
        name="moe_experts",
    )(block_e, n_used, xs, w1, b1, w2, b2)


def _combine_kernel(pstart_ref, ploc_hbm, tmeta_hbm, ys_hbm, x1_ref, gate_ref, lg_ref, lb_ref, o_ref, ob_ref,
                    psm, msm, sbuf, gbuf, isem, rsem, *, tc, sub, T, alpha):
    i = pl.program_id(0)
    r = x1_ref.shape[0] // tc
    n_sub = tc // sub
    idx_copies = [pltpu.make_async_copy(ploc_hbm.at[pl.ds(k * T + i * tc, tc)], psm.at[pl.ds(k * tc, tc)], isem)
                  for k in range(TOP_K)]
    idx_copies.append(pltpu.make_async_copy(tmeta_hbm.at[pl.ds(i * n_sub, n_sub)], msm, isem))
    for c in idx_copies:
        c.start()
    for c in idx_copies:
        c.wait()

    def fetch(j):
        _tile_runs(j, pstart_ref, msm, sbuf.at[j % 2], ys_hbm, rsem.at[j % 2], r, sub, False)

    fetch(0)
    for j in range(n_sub):
        slot = j % 2
        if j + 1 < n_sub:
            fetch(j + 1)
        pltpu.make_async_copy(ys_hbm.at[pl.ds(0, TOP_K * sub * r)], sbuf.at[slot], rsem.at[slot]).wait()

        def pick(t, c):
            for k in range(TOP_K):
                gbuf[k, pl.ds(pl.multiple_of(t * r, r), r), :] = sbuf[
                    slot, pl.ds(pl.multiple_of(psm[k * tc + j * sub + t], r), r), :]
            return c

        lax.fori_loop(0, sub, pick, 0, unroll=8)
        rows = slice(j * sub, (j + 1) * sub)
        gate = gate_ref[rows, :]
        y = gate[:, 0:1] * _rows_from_tiles(gbuf.at[0], sub)
        for k in range(1, TOP_K):
            y = y + gate[:, k:k + 1] * _rows_from_tiles(gbuf.at[k], sub)
        x1 = _rows_from_tiles(x1_ref.at[pl.ds(j * sub * r, sub * r)], sub)
        out = _ln_rows(alpha * x1 + y, lg_ref[...], lb_ref[...])
        o_ref[rows, :] = out
        ob_ref[rows, :] = out.astype(BF16)


def _combine(pstart, ploc_flat, tmeta, ys, x1t, gate_t, lg, lb, tc, sub, alpha):
    T = gate_t.shape[0]
    D = lg.shape[1]
    r = D // LANES_V7X
    const = lambda i, ps: (0, 0)
    row = lambda i, ps: (i, 0)
    return pl.pallas_call(
        functools.partial(_combine_kernel, tc=tc, sub=sub, T=T, alpha=alpha),
        grid_spec=pltpu.PrefetchScalarGridSpec(
            num_scalar_prefetch=1,
            grid=(T // tc,),
            in_specs=[
                pl.BlockSpec(memory_space=pl.ANY),
                pl.BlockSpec(memory_space=pl.ANY),
                pl.BlockSpec(memory_space=pl.ANY),
                pl.BlockSpec((tc * r, LANES_V7X), row),
                pl.BlockSpec((tc, TOP_K), row),
                pl.BlockSpec((1, D), const),
                pl.BlockSpec((1, D), const),
            ],
            out_specs=[pl.BlockSpec((tc, D), row), pl.BlockSpec((tc, D), row)],
            scratch_shapes=[
                pltpu.SMEM((TOP_K * tc,), jnp.int32),
                pltpu.SMEM((tc // sub, SUBLANES_V7X, LANES_V7X), jnp.int32),
                pltpu.VMEM((2, TOP_K * sub * r, LANES_V7X), F32),
                pltpu.VMEM((TOP_K, sub * r, LANES_V7X), F32),
                pltpu.SemaphoreType.DMA,
                pltpu.SemaphoreType.DMA((2,)),
            ],
        ),
        out_shape=[jax.ShapeDtypeStruct((T, D), F32), jax.ShapeDtypeStruct((T, D), BF16)],
        compiler_params=_cparams(("arbitrary",)),
        name="moe_combine",
    )(pstart, ploc_flat, tmeta, ys, x1t, gate_t, lg, lb)


def _tile(n, pref):
    t = min(n, pref)
    assert n % t == 0, (n, pref)
    return t


def kernel(x, w_in, b_in, conv_dw_w, conv_dw_b, conv_norm_g, conv_norm_b, conv_out_w, conv_out_b, qk_conv_w, qk_conv_b, mlstm_norm_g, mlstm_out_w, w_out, ln1_g, ln1_b, router_w, router_b, moe_w1, moe_b1, moe_w2, moe_b2, ln2_g, ln2_b):
    B, S, D = x.shape
    depth, _, n_in = w_in.shape
    C = conv_dw_w.shape[2]
    W = mlstm_out_w.shape[1]
    E = router_w.shape[2]
    heads = (n_in - 2 * C - 4 * W - 2 * D) // 2
    assert C == D and W == D and n_in == 2 * C + 4 * W + 2 * heads + 2 * D
    T = B * S
    alpha = float((2 * depth) ** 0.25)
    off_i = 2 * C + 4 * W

    tm_proj = _tile(T, 1024)
    ts_conv = _tile(S, 256)
    l_chunk = _tile(S, 256)
    td = _tile(T, 1024)
    sub = _tile(td, 256)
    tm_merge = _tile(T, 2 * sub)
    blk = _tile(T * TOP_K, 512)
    n_blocks = T * TOP_K // blk + E
    n_rows = n_blocks * blk

    tn_proj = _tile(2 * D, 1024)
    assert off_i % tn_proj == 0 and 2 * heads < LANES_V7X
    b_main = jnp.concatenate([b_in[:, :off_i], b_in[:, off_i + 2 * heads:]], axis=1)[:, None, :]
    w_if = jnp.pad(w_in[:, :, off_i:off_i + 2 * heads], ((0, 0), (0, 0), (0, LANES_V7X - 2 * heads))).astype(BF16)
    b_if = jnp.pad(b_in[:, off_i:off_i + 2 * heads], ((0, 0), (0, LANES_V7X - 2 * heads)))[:, None, :]
    conv_ow = conv_out_w.astype(BF16)
    ml_ow = mlstm_out_w.astype(BF16)
    wo = w_out.astype(BF16)
    rwt = jnp.swapaxes(router_w, 1, 2).astype(BF16)
    rb = jnp.broadcast_to(router_b[:, :, None], (depth, E, LANES_V7X))

    xf = x.reshape(T, D)
    xb = xf.astype(BF16)
    for l in range(depth):
        xp = _linear(xb, w_in, l, b_main[l], off_i // tn_proj, (n_in - 2 * heads) // tn_proj, 2 * heads,
                     tm_proj, tn_proj)
        yc = _conformer(xp, conv_dw_w[l], conv_dw_b[l][None], conv_norm_g[l][None], conv_norm_b[l][None],
                        conv_ow[l], conv_out_b[l][None], B, S, ts_conv)
        ym = _mlstm(xp, xb, w_if[l], b_if[l], qk_conv_w[l][:, :W], qk_conv_b[l][None, :W],
                    qk_conv_w[l][:, W:], qk_conv_b[l][None, W:], mlstm_norm_g[l][None], ml_ow[l],
                    B, S, l_chunk, heads, 2 * C // W)
        x1t, gate, ploc, tmeta, counts = _merge(xp, yc, ym, xf, wo[l], ln1_g[l][None], ln1_b[l][None], rwt[l],
                                                rb[l], tm_merge, sub, (2 * C + 4 * W) // D, alpha)
        blocks, meta = _plan(counts, blk, n_blocks)
        ploc = ploc.reshape(TOP_K * T)
        xs = _dispatch(meta[0, :E + 1], meta[1, :E], ploc, tmeta, x1t, T, n_rows, td, sub, blk)
        ys = _experts(blocks[0, :n_blocks], meta[2, :1], xs, moe_w1[l], moe_b1[l][:, None, :], moe_w2[l],
                      moe_b2[l][:, None, :], blk)
        xf, xb = _combine(meta[1, :E], ploc, tmeta, ys, x1t, gate.T, ln2_g[l][None], ln2_b[l][None], td, sub, alpha)
    return xf.reshape(B, S, D)
```

```python
import functools

import jax
import jax.numpy as jnp
from jax import lax
from jax.experimental import pallas as pl
from jax.experimental.pallas import tpu as pltpu

TOP_K = 4
SWIGLU_LIMIT = 7.0
SWIGLU_ALPHA = 1.702
LN_EPS = 1e-5
LANES_V7X = 128
SUBLANES_V7X = 8
VMEM_LIMIT_V7X = 56 * 1024 * 1024

F32 = jnp.float32
BF16 = jnp.bfloat16


def _cparams(sem):
    return pltpu.CompilerParams(dimension_semantics=sem, vmem_limit_bytes=VMEM_LIMIT_V7X)


def _ln_rows(x, g, b):
    mu = jnp.mean(x, axis=-1, keepdims=True)
    xc = x - mu
    var = jnp.mean(xc * xc, axis=-1, keepdims=True)
    y = xc * lax.rsqrt(var + LN_EPS) * g
    return y if b is None else y + b


def _sigmoid(x):
    return 1.0 / (1.0 + jnp.exp(-x))


def _log_sigmoid(x):
    return jnp.minimum(x, 0.0) - jnp.log(1.0 + jnp.exp(-jnp.abs(x)))


def _rows_from_tiles(ref, n):
    r = ref.shape[0] // n
    return jnp.concatenate([ref[pl.ds(s, n, stride=r), :] for s in range(r)], axis=1)


def _rows_to_tiles(ref, val):
    n = val.shape[0]
    r = ref.shape[0] // n
    for s in range(r):
        ref[pl.ds(s, n, stride=r), :] = val[:, s * LANES_V7X:(s + 1) * LANES_V7X]


def _linear_kernel(x_ref, wlo_ref, whi_ref, b_ref, o_ref, wbf, *, n_aligned, skip, kc):
    j = pl.program_id(0)
    i = pl.program_id(1)
    K, tn = wbf.shape

    @pl.when(jnp.logical_and(i == 0, j < n_aligned))
    def _():
        wbf[...] = wlo_ref[0].astype(BF16)

    @pl.when(jnp.logical_and(i == 0, j >= n_aligned))
    def _():
        for r0 in range(0, K, kc):
            both = jnp.concatenate([wlo_ref[0, r0:r0 + kc, :], whi_ref[0, r0:r0 + kc, :]], axis=1)
            wbf[r0:r0 + kc, :] = pltpu.roll(both, 2 * tn - skip, axis=1)[:, :tn].astype(BF16)

    o_ref[...] = (jnp.dot(x_ref[...], wbf[...], preferred_element_type=F32) + b_ref[...]).astype(o_ref.dtype)


def _linear(x, w_all, layer, b, n_aligned, n_blocks, skip, tm, tn):
    T, K = x.shape
    last = -(-w_all.shape[2] // tn) - 1
    return pl.pallas_call(
        functools.partial(_linear_kernel, n_aligned=n_aligned, skip=skip, kc=min(K, 256)),
        grid=(n_blocks, T // tm),
        in_specs=[
            pl.BlockSpec((tm, K), lambda j, i: (i, 0)),
            pl.BlockSpec((1, K, tn), lambda j, i: (layer, 0, j)),
            pl.BlockSpec((1, K, tn), lambda j, i: (layer, 0, jnp.minimum(j + 1, last))),
            pl.BlockSpec((1, tn), lambda j, i: (0, j)),
        ],
        out_specs=pl.BlockSpec((tm, tn), lambda j, i: (i, j)),
        out_shape=jax.ShapeDtypeStruct((T, n_blocks * tn), BF16),
        scratch_shapes=[pltpu.VMEM((K, tn), BF16)],
        compiler_params=_cparams(("arbitrary", "arbitrary")),
        name="in_proj",
    )(x, w_all, w_all, b)


def _conformer_kernel(a_ref, g_ref, dww_ref, dwb_ref, ng_ref, nb_ref, ow_ref, ob_ref, y_ref,
                      zbuf, cbuf, *, taps, halo, rb, cb):
    s = pl.program_id(1)
    ts, C = a_ref.shape

    @pl.when(s == 0)
    def _():
        zbuf[0:halo, :] = jnp.zeros((halo, C), F32)

    @pl.when(s != 0)
    def _():
        zbuf[0:halo, :] = zbuf[ts:ts + halo, :]

    zbuf[halo:halo + ts, :] = a_ref[...].astype(F32) * _sigmoid(g_ref[...].astype(F32))
    off = halo - (taps - 1)
    sub = SUBLANES_V7X
    for r0 in range(0, ts, rb):
        for c0 in range(0, C, cb):
            acc = jnp.broadcast_to(dwb_ref[:, c0:c0 + cb], (rb, cb))
            for r in range(sub):
                wl = rb + (sub if r else 0)
                q = None
                for a in range((off + taps - 1) // sub + 1):
                    j = sub * a + r - off
                    if 0 <= j < taps:
                        assert r0 + sub * a + wl <= halo + ts
                        term = dww_ref[j:j + 1, c0:c0 + cb] * zbuf[r0 + sub * a:r0 + sub * a + wl, c0:c0 + cb]
                        q = term if q is None else q + term
                if q is not None:
                    acc = acc + q[r:r + rb, :]
            cbuf[r0:r0 + rb, c0:c0 + cb] = acc
    zn = _ln_rows(cbuf[...], ng_ref[...], nb_ref[...])
    act = zn * _sigmoid(zn)
    y_ref[...] = jnp.dot(act.astype(BF16), ow_ref[...], preferred_element_type=F32) + ob_ref[...]


def _conformer(xp, dww, dwb, ng, nb, ow, ob, B, S, ts):
    T = xp.shape[0]
    C = dww.shape[1]
    D = ow.shape[1]
    taps = dww.shape[0]
    halo = -(-(taps - 1) // SUBLANES_V7X) * SUBLANES_V7X
    nS = S // ts
    kern = functools.partial(_conformer_kernel, taps=taps, halo=halo, rb=min(64, ts), cb=min(256, C))
    const = lambda b, s: (0, 0)
    return pl.pallas_call(
        kern,
        grid=(B, nS),
        in_specs=[
            pl.BlockSpec((ts, C), lambda b, s: (b * nS + s, 0)),
            pl.BlockSpec((ts, C), lambda b, s: (b * nS + s, 1)),
            pl.BlockSpec((taps, C), const),
            pl.BlockSpec((1, C), const),
            pl.BlockSpec((1, C), const),
            pl.BlockSpec((1, C), const),
            pl.BlockSpec((C, D), const),
            pl.BlockSpec((1, D), const),
        ],
        out_specs=pl.BlockSpec((ts, D), lambda b, s: (b * nS + s, 0)),
        out_shape=jax.ShapeDtypeStruct((T, D), F32),
        scratch_shapes=[pltpu.VMEM((halo + ts, C), F32), pltpu.VMEM((ts, C), F32)],
        compiler_params=_cparams(("arbitrary", "arbitrary")),
        name="conformer",
    )(xp, xp, dww, dwb, ng, nb, ow, ob)


def _mlstm_kernel(q_ref, k_ref, v_ref, o_ref, x_ref, wif_ref, bif_ref, wq_ref, bq_ref, wk_ref, bk_ref,
                  ng_ref, ow_ref, y_ref, qbuf, kbuf, c_ref, n_ref, m_ref, hbuf, *, heads, taps, halo):
    s = pl.program_id(1)
    L, W = q_ref.shape
    dh = W // heads

    @pl.when(s == 0)
    def _():
        qbuf[0:halo, :] = jnp.zeros((halo, W), F32)
        kbuf[0:halo, :] = jnp.zeros((halo, W), F32)
        c_ref[...] = jnp.zeros(c_ref.shape, F32)
        n_ref[...] = jnp.zeros(n_ref.shape, F32)
        m_ref[...] = jnp.zeros(m_ref.shape, F32)

    @pl.when(s != 0)
    def _():
        qbuf[0:halo, :] = qbuf[L:L + halo, :]
        kbuf[0:halo, :] = kbuf[L:L + halo, :]

    qbuf[halo:halo + L, :] = q_ref[...].astype(F32)
    kbuf[halo:halo + L, :] = k_ref[...].astype(F32)
    off = halo - (taps - 1)

    def short_conv(buf, w_ref, b_ref):
        acc = b_ref[...] + w_ref[0:1, :] * buf[off:off + L, :]
        for j in range(1, taps):
            acc = acc + w_ref[j:j + 1, :] * buf[off + j:off + j + L, :]
        return acc * _sigmoid(acc)

    qc = short_conv(qbuf, wq_ref, bq_ref)
    kc = short_conv(kbuf, wk_ref, bk_ref) * (dh ** -0.5)

    gates = jnp.dot(x_ref[...], wif_ref[...], preferred_element_type=F32) + bif_ref[...]
    logf = _log_sigmoid(gates)
    row = lax.broadcasted_iota(jnp.int32, (L, L), 0)
    col = lax.broadcasted_iota(jnp.int32, (L, L), 1)
    causal = row >= col
    bcum = jnp.dot(causal.astype(F32), logf, preferred_element_type=F32, precision=lax.Precision.HIGHEST)
    gates_t = gates.T
    bcum_t = bcum.T

    for h in range(heads):
        sl = slice(h * dh, (h + 1) * dh)
        qf = qc[:, sl]
        kf = kc[:, sl]
        qh = qf.astype(BF16)
        kh = kf.astype(BF16)
        vb = v_ref[:, sl]
        vf = vb.astype(F32)
        ig_c = gates[:, h:h + 1]
        ig_r = gates_t[h:h + 1, :]
        b_c = bcum[:, heads + h:heads + h + 1]
        b_r = bcum_t[heads + h:heads + h + 1, :]
        m_prev = m_ref[h:h + 1, 0:1]
        dmat = jnp.where(causal, b_c - b_r + ig_r, -jnp.inf)
        inter = b_c + m_prev
        m_row = jnp.maximum(inter, jnp.max(dmat, axis=1, keepdims=True))
        w_intra = jnp.exp(dmat - m_row)
        w_inter = jnp.exp(inter - m_row)
        sc = lax.dot_general(qh, kh, (((1,), (1,)), ((), ())), preferred_element_type=F32) * w_intra
        cmat = c_ref[h]
        num = w_inter * jnp.dot(qh, cmat.astype(BF16), preferred_element_type=F32) + jnp.dot(
            sc.astype(BF16), vb, preferred_element_type=F32)
        nvec = n_ref[h:h + 1, :]
        den = w_inter * jnp.sum(qf * nvec, axis=1, keepdims=True) + jnp.sum(sc, axis=1, keepdims=True)
        hh = num / jnp.maximum(jnp.abs(den), jnp.exp(-m_row))
        b_last = b_c[L - 1:L, :]
        lws_c = b_last - b_c + ig_c
        lws_r = b_last - b_r + ig_r
        m_new = jnp.maximum(b_last + m_prev, jnp.max(lws_r, axis=1, keepdims=True))
        ws_c = jnp.exp(lws_c - m_new)
        decay = jnp.exp(b_last + m_prev - m_new)
        kv = lax.dot_general(kh, (vf * ws_c).astype(BF16), (((0,), (0,)), ((), ())), preferred_element_type=F32)
        c_ref[h] = decay * cmat + kv
        n_ref[h:h + 1, :] = decay * nvec + jnp.sum(kf * ws_c, axis=0, keepdims=True)
        m_ref[h:h + 1, :] = jnp.broadcast_to(m_new, (1, m_ref.shape[1]))
        hn = _ln_rows(hh, ng_ref[:, sl], None)
        hbuf[:, sl] = hn * _sigmoid(o_ref[:, sl].astype(F32))
    y_ref[...] = jnp.dot(hbuf[...].astype(BF16), ow_ref[...], preferred_element_type=F32)


def _mlstm(xp, xb, wif, bif, wq, bq, wk, bk, ng, ow, B, S, L, heads, col_q):
    T = xp.shape[0]
    D = xb.shape[1]
    W = wq.shape[1]
    taps = wq.shape[0]
    dh = W // heads
    halo = SUBLANES_V7X
    nS = S // L
    kern = functools.partial(_mlstm_kernel, heads=heads, taps=taps, halo=halo)
    const = lambda b, s: (0, 0)

    def colspec(c):
        return pl.BlockSpec((L, W), lambda b, s: (b * nS + s, c))

    return pl.pallas_call(
        kern,
        grid=(B, nS),
        in_specs=[
            colspec(col_q), colspec(col_q + 1), colspec(col_q + 2), colspec(col_q + 3),
            pl.BlockSpec((L, D), lambda b, s: (b * nS + s, 0)),
            pl.BlockSpec(wif.shape, const),
            pl.BlockSpec(bif.shape, const),
            pl.BlockSpec((taps, W), const),
            pl.BlockSpec((1, W), const),
            pl.BlockSpec((taps, W), const),
            pl.BlockSpec((1, W), const),
            pl.BlockSpec((1, W), const),
            pl.BlockSpec((W, D), const),
        ],
        out_specs=pl.BlockSpec((L, D), lambda b, s: (b * nS + s, 0)),
        out_shape=jax.ShapeDtypeStruct((T, D), F32),
        scratch_shapes=[
            pltpu.VMEM((halo + L, W), F32),
            pltpu.VMEM((halo + L, W), F32),
            pltpu.VMEM((heads, dh, dh), F32),
            pltpu.VMEM((heads, dh), F32),
            pltpu.VMEM((heads, LANES_V7X), F32),
            pltpu.VMEM((L, W), F32),
        ],
        compiler_params=_cparams(("arbitrary", "arbitrary")),
        name="mlstm",
    )(xp, xp, xp, xp, xb, wif, bif, wq, bq, wk, bk, ng, ow)


def _merge_kernel(ga_ref, gb_ref, yc_ref, ym_ref, x_ref, wo_ref, lg_ref, lb_ref, rwt_ref, rb_ref,
                  x1_ref, gate_ref, ploc_ref, tmeta_ref, cnt_ref, carry, *, alpha, sub):
    i = pl.program_id(0)
    tm = x_ref.shape[0]
    E = rwt_ref.shape[0]
    rpt = x1_ref.shape[0] // tm

    @pl.when(i == 0)
    def _():
        carry[...] = jnp.zeros(carry.shape, F32)

    mix = _sigmoid(ga_ref[...].astype(F32)) * yc_ref[...] + _sigmoid(gb_ref[...].astype(F32)) * ym_ref[...]
    r = alpha * x_ref[...] + jnp.dot(mix.astype(BF16), wo_ref[...], preferred_element_type=F32)
    x1 = _ln_rows(r, lg_ref[...], lb_ref[...])
    _rows_to_tiles(x1_ref, x1)
    logits = lax.dot_general(rwt_ref[...], x1.astype(BF16), (((1,), (1,)), ((), ())),
                             preferred_element_type=F32) + rb_ref[:, 0:1]
    eidx = lax.broadcasted_iota(jnp.int32, (E, tm), 0).astype(F32)
    work = logits
    tops, hots = [], []
    for k in range(TOP_K):
        mk = jnp.max(work, axis=0, keepdims=True)
        ik = jnp.min(jnp.where(work == mk, eidx, float(E)), axis=0, keepdims=True)
        hot = eidx == ik
        work = jnp.where(hot, -jnp.inf, work)
        tops.append(mk)
        hots.append(hot)
    exps = [jnp.exp(t - tops[0]) for t in tops]
    denom = exps[0]
    for e in exps[1:]:
        denom = denom + e
    for k in range(TOP_K):
        gate_ref[k:k + 1, :] = exps[k] / denom
    trow = lax.broadcasted_iota(jnp.int32, (sub, sub), 0)
    tcol = lax.broadcasted_iota(jnp.int32, (sub, sub), 1)
    before = (trow < tcol).astype(BF16)
    er = lax.broadcasted_iota(jnp.int32, (E, E), 0)
    ec = lax.broadcasted_iota(jnp.int32, (E, E), 1)
    tmeta_ref[...] = jnp.zeros(tmeta_ref.shape, jnp.int32)
    seen = carry[...]
    for u in range(tm // sub):
        lanes = slice(u * sub, (u + 1) * sub)
        tile_hots = [hot[:, lanes] for hot in hots]
        sel = tile_hots[0].astype(F32)
        for hot in tile_hots[1:]:
            sel = sel + hot.astype(F32)
        earlier = jnp.dot(sel.astype(BF16), before, preferred_element_type=F32)
        cnt_tile = jnp.sum(sel, axis=1, keepdims=True)
        cnt_row = jnp.sum(jnp.where(er == ec, cnt_tile, 0.0), axis=0, keepdims=True)
        pos = jnp.sum(jnp.where(ec < er, cnt_row, 0.0), axis=1, keepdims=True) + earlier
        for k in range(TOP_K):
            ploc_ref[k:k + 1, lanes] = (
                jnp.sum(jnp.where(tile_hots[k], pos, 0.0), axis=0, keepdims=True) * rpt).astype(jnp.int32)
        seen_row = jnp.sum(jnp.where(er == ec, seen[:, 0:1], 0.0), axis=0, keepdims=True)
        tmeta_ref[u, 0:1, 0:E] = cnt_row.astype(jnp.int32)
        tmeta_ref[u, 1:2, 0:E] = seen_row.astype(jnp.int32)
        seen = seen + cnt_tile
    carry[...] = seen
    cnt_ref[...] = seen


def _merge(xp, yc, ym, x, wo, lg, lb, rwt, rb, tm, sub, col_ga, alpha):
    T, D = x.shape
    E = rwt.shape[0]
    kern = functools.partial(_merge_kernel, alpha=alpha, sub=sub)
    const = lambda i: (0, 0)
    row = lambda i: (i, 0)
    krow = pl.BlockSpec((TOP_K, tm), lambda i: (0, i))
    return pl.pallas_call(
        kern,
        grid=(T // tm,),
        in_specs=[
            pl.BlockSpec((tm, D), lambda i: (i, col_ga)),
            pl.BlockSpec((tm, D), lambda i: (i, col_ga + 1)),
            pl.BlockSpec((tm, D), row),
            pl.BlockSpec((tm, D), row),
            pl.BlockSpec((tm, D), row),
            pl.BlockSpec((D, D), const),
            pl.BlockSpec((1, D), const),
            pl.BlockSpec((1, D), const),
            pl.BlockSpec((E, D), const),
            pl.BlockSpec((E, LANES_V7X), const),
        ],
        out_specs=[
            pl.BlockSpec((tm * D // LANES_V7X, LANES_V7X), row),
            krow, krow,
            pl.BlockSpec((tm // sub, SUBLANES_V7X, LANES_V7X), lambda i: (i, 0, 0)),
            pl.BlockSpec((E, LANES_V7X), const),
        ],
        out_shape=[
            jax.ShapeDtypeStruct((T * D // LANES_V7X, LANES_V7X), F32),
            jax.ShapeDtypeStruct((TOP_K, T), F32),
            jax.ShapeDtypeStruct((TOP_K, T), jnp.int32),
            jax.ShapeDtypeStruct((T // sub, SUBLANES_V7X, LANES_V7X), jnp.int32),
            jax.ShapeDtypeStruct((E, LANES_V7X), F32),
        ],
        scratch_shapes=[pltpu.VMEM((E, LANES_V7X), F32)],
        compiler_params=_cparams(("arbitrary",)),
        name="merge_router",
    )(xp, xp, yc, ym, x, wo, lg, lb, rwt, rb)


def _plan_kernel(cnt_ref, blk_ref, meta_ref, *, blk):
    E = cnt_ref.shape[0]
    nbp = blk_ref.shape[1]
    cnt = cnt_ref[:, 0:1]
    padded = jnp.floor((cnt + (blk - 1)) / blk) * blk
    er = lax.broadcasted_iota(jnp.int32, (E, E), 0)
    ec = lax.broadcasted_iota(jnp.int32, (E, E), 1)
    padded_row = jnp.sum(jnp.where(er == ec, padded, 0.0), axis=0, keepdims=True)
    pstart = jnp.sum(jnp.where(ec < er, padded_row, 0.0), axis=1, keepdims=True)
    pend = pstart + padded
    pend_row = jnp.sum(jnp.where(er <= ec, padded, 0.0), axis=0, keepdims=True)
    n_used = jnp.sum(padded_row, axis=1, keepdims=True) / blk
    meta_ref[...] = jnp.broadcast_to(n_used, meta_ref.shape).astype(jnp.int32)
    meta_ref[0:1, 0:E] = pend_row.astype(jnp.int32)
    meta_ref[1:2, 0:E] = (pend_row - padded_row).astype(jnp.int32)
    first_row = lax.broadcasted_iota(jnp.int32, (E, nbp), 1).astype(F32) * blk
    inside = jnp.logical_and(pstart <= first_row, first_row < pend)
    eid = lax.broadcasted_iota(jnp.int32, (E, nbp), 0).astype(F32)
    block_e = jnp.sum(jnp.where(inside, eid, 0.0), axis=0, keepdims=True)
    valid = jnp.sum(jnp.where(inside, 1.0, 0.0), axis=0, keepdims=True)
    e_last = jnp.max(jnp.where(padded > 0.0, eid[:, 0:1], 0.0), axis=0, keepdims=True)
    block_e = jnp.where(valid > 0.0, block_e, e_last)
    blk_ref[...] = jnp.broadcast_to(block_e.astype(jnp.int32), blk_ref.shape)


def _plan(counts, blk, n_blocks):
    E = counts.shape[0]
    nbp = -(-n_blocks // LANES_V7X) * LANES_V7X
    return pl.pallas_call(
        functools.partial(_plan_kernel, blk=blk),
        out_shape=[
            jax.ShapeDtypeStruct((SUBLANES_V7X, nbp), jnp.int32),
            jax.ShapeDtypeStruct((SUBLANES_V7X, LANES_V7X), jnp.int32),
        ],
        name="route_plan",
    )(counts)


def _run_copies(n, local_row, global_row, buf, hbm, sem, rpt, max_rows, to_hbm):
    for b in range(max_rows.bit_length()):
        size = 1 << b
        done = n & ~(2 * size - 1)

        @pl.when((n & size) != 0)
        def _():
            loc = buf.at[pl.ds(pl.multiple_of((local_row + done) * rpt, rpt), size * rpt)]
            glob = hbm.at[pl.ds(pl.multiple_of((global_row + done) * rpt, rpt), size * rpt)]
            src, dst = (loc, glob) if to_hbm else (glob, loc)
            pltpu.make_async_copy(src, dst, sem).start()


def _tile_runs(j, pstart_ref, msm, buf, hbm, sem, rpt, sub, to_hbm):
    def runs(e, off):
        n = msm[j, 0, e]
        _run_copies(n, off, pstart_ref[e] + msm[j, 1, e], buf, hbm, sem, rpt, sub, to_hbm)
        return off + n

    lax.fori_loop(0, pstart_ref.shape[0], runs, 0)


def _dispatch_kernel(pend_ref, pstart_ref, ploc_hbm, tmeta_hbm, x_ref, xs_hbm, psm, msm, sbuf, zbuf, isem, rsem, zsem,
                     *, td, sub, T, blk, n_blocks):
    i = pl.program_id(0)
    E = pend_ref.shape[0] - 1
    r = x_ref.shape[0] // td
    n_sub = td // sub
    idx_copies = [pltpu.make_async_copy(ploc_hbm.at[pl.ds(k * T + i * td, td)], psm.at[pl.ds(k * td, td)], isem)
                  for k in range(TOP_K)]
    idx_copies.append(pltpu.make_async_copy(tmeta_hbm.at[pl.ds(i * n_sub, n_sub)], msm, isem))
    for c in idx_copies:
        c.start()

    @pl.when(i == 0)
    def _():
        zbuf[...] = jnp.zeros(zbuf.shape, zbuf.dtype)
        n_used = pend_ref[E]

        def clear(start):
            return pltpu.make_async_copy(zbuf, xs_hbm.at[pl.ds(pl.multiple_of(start * r, blk * r), blk * r)], zsem)

        def clears(fn):
            for e in range(E):
                @pl.when(pend_ref[e] >= blk)
                def _():
                    fn(clear(pend_ref[e] - blk))

                @pl.when(n_used + e < n_blocks)
                def _():
                    fn(clear((n_used + e) * blk))

        clears(lambda c: c.start())
        clears(lambda c: c.wait())

    for c in idx_copies:
        c.wait()

    def drain(slot):
        pltpu.make_async_copy(sbuf.at[slot], xs_hbm.at[pl.ds(0, TOP_K * sub * r)], rsem.at[slot]).wait()

    for j in range(n_sub):
        slot = j % 2

        def place(t, c):
            row = x_ref[pl.ds(pl.multiple_of((j * sub + t) * r, r), r), :]
            for k in range(TOP_K):
                sbuf[slot, pl.ds(pl.multiple_of(psm[k * td + j * sub + t], r), r), :] = row
            return c

        lax.fori_loop(0, sub, place, 0, unroll=8)
        _tile_runs(j, pstart_ref, msm, sbuf.at[slot], xs_hbm, rsem.at[slot], r, sub, True)
        if j >= 1:
            drain(1 - slot)
    drain((n_sub - 1) % 2)


def _dispatch(pend, pstart, ploc_flat, tmeta, x1t, T, n_rows, td, sub, blk):
    r = x1t.shape[0] // T
    return pl.pallas_call(
        functools.partial(_dispatch_kernel, td=td, sub=sub, T=T, blk=blk, n_blocks=n_rows // blk),
        grid_spec=pltpu.PrefetchScalarGridSpec(
            num_scalar_prefetch=2,
            grid=(T // td,),
            in_specs=[
                pl.BlockSpec(memory_space=pl.ANY),
                pl.BlockSpec(memory_space=pl.ANY),
                pl.BlockSpec((td * r, LANES_V7X), lambda i, pend, pstart: (i, 0)),
            ],
            out_specs=pl.BlockSpec(memory_space=pl.ANY),
            scratch_shapes=[
                pltpu.SMEM((TOP_K * td,), jnp.int32),
                pltpu.SMEM((td // sub, SUBLANES_V7X, LANES_V7X), jnp.int32),
                pltpu.VMEM((2, TOP_K * sub * r, LANES_V7X), F32),
                pltpu.VMEM((blk * r, LANES_V7X), F32),
                pltpu.SemaphoreType.DMA,
                pltpu.SemaphoreType.DMA((2,)),
                pltpu.SemaphoreType.DMA,
            ],
        ),
        out_shape=jax.ShapeDtypeStruct((n_rows * r, LANES_V7X), F32),
        compiler_params=_cparams(("arbitrary",)),
        name="moe_dispatch",
    )(pend, pstart, ploc_flat, tmeta, x1t)


def _expert_kernel(be_ref, nu_ref, xs_ref, w1_ref, b1_ref, w2_ref, b2_ref, y_ref, w1b, w2b):
    i = pl.program_id(0)
    D, F2 = w1b.shape
    F = F2 // 2

    @pl.when(jnp.logical_or(i == 0, be_ref[i] != be_ref[jnp.maximum(i - 1, 0)]))
    def _():
        w1b[...] = w1_ref[0, 0].astype(BF16)
        w2b[...] = w2_ref[0, 0].astype(BF16)

    @pl.when(i < nu_ref[0])
    def _():
        xs = _rows_from_tiles(xs_ref, xs_ref.shape[0] * LANES_V7X // D)
        h = jnp.dot(xs.astype(BF16), w1b[...], preferred_element_type=F32) + b1_ref[0, 0]
        g = jnp.minimum(h[:, :F], SWIGLU_LIMIT)
        lin = jnp.clip(h[:, F:], -SWIGLU_LIMIT, SWIGLU_LIMIT)
        act = g * _sigmoid(SWIGLU_ALPHA * g) * (lin + 1.0)
        _rows_to_tiles(y_ref, jnp.dot(act.astype(BF16), w2b[...], preferred_element_type=F32) + b2_ref[0, 0])

    @pl.when(i >= nu_ref[0])
    def _():
        y_ref[...] = jnp.zeros(y_ref.shape, F32)


def _experts(block_e, n_used, xs, w1, b1, w2, b2, layer, blk):
    _, E, D, F2 = w1.shape
    F = w2.shape[2]
    r = D // LANES_V7X
    NR = xs.shape[0] // r
    wmap = lambda i, be, nu: (layer, be[i], 0, 0)
    return pl.pallas_call(
        _expert_kernel,
        grid_spec=pltpu.PrefetchScalarGridSpec(
            num_scalar_prefetch=2,
            grid=(NR // blk,),
            in_specs=[
                pl.BlockSpec((blk * r, LANES_V7X), lambda i, be, nu: (jnp.minimum(i, nu[0] - 1), 0)),
                pl.BlockSpec((1, 1, D, F2), wmap),
                pl.BlockSpec((1, 1, 1, F2), wmap),
                pl.BlockSpec((1, 1, F, D), wmap),
                pl.BlockSpec((1, 1, 1, D), wmap),
            ],
            out_specs=pl.BlockSpec((blk * r, LANES_V7X), lambda i, be, nu: (i, 0)),
            scratch_shapes=[pltpu.VMEM((D, F2), BF16), pltpu.VMEM((F, D), BF16)],
        ),
        out_shape=jax.ShapeDtypeStruct((NR * r, LANES_V7X), F32),
        compiler_params=_cparams(("arbitrary",)),
        name="moe_experts",
    )(block_e, n_used, xs, w1, b1, w2, b2)


def _combine_kernel(pstart_ref, ploc_hbm, tmeta_hbm, ys_hbm, x1_ref, gate_ref, lg_ref, lb_ref, o_ref, ob_ref,
                    psm, msm, sbuf, gbuf, isem, rsem, *, tc, sub, T, alpha):
    i = pl.program_id(0)
    r = x1_ref.shape[0] // tc
    n_sub = tc // sub
    idx_copies = [pltpu.make_async_copy(ploc_hbm.at[pl.ds(k * T + i * tc, tc)], psm.at[pl.ds(k * tc, tc)], isem)
                  for k in range(TOP_K)]
    idx_copies.append(pltpu.make_async_copy(tmeta_hbm.at[pl.ds(i * n_sub, n_sub)], msm, isem))
    for c in idx_copies:
        c.start()
    for c in idx_copies:
        c.wait()

    def fetch(j):
        _tile_runs(j, pstart_ref, msm, sbuf.at[j % 2], ys_hbm, rsem.at[j % 2], r, sub, False)

    fetch(0)
    for j in range(n_sub):
        slot = j % 2
        if j + 1 < n_sub:
            fetch(j + 1)
        pltpu.make_async_copy(ys_hbm.at[pl.ds(0, TOP_K * sub * r)], sbuf.at[slot], rsem.at[slot]).wait()

        def pick(t, c):
            for k in range(TOP_K):
                gbuf[k, pl.ds(pl.multiple_of(t * r, r), r), :] = sbuf[
                    slot, pl.ds(pl.multiple_of(psm[k * tc + j * sub + t], r), r), :]
            return c

        lax.fori_loop(0, sub, pick, 0, unroll=8)
        rows = slice(j * sub, (j + 1) * sub)
        gate = gate_ref[rows, :]
        y = gate[:, 0:1] * _rows_from_tiles(gbuf.at[0], sub)
        for k in range(1, TOP_K):
            y = y + gate[:, k:k + 1] * _rows_from_tiles(gbuf.at[k], sub)
        x1 = _rows_from_tiles(x1_ref.at[pl.ds(j * sub * r, sub * r)], sub)
        out = _ln_rows(alpha * x1 + y, lg_ref[...], lb_ref[...])
        o_ref[rows, :] = out
        ob_ref[rows, :] = out.astype(BF16)


def _combine(pstart, ploc_flat, tmeta, ys, x1t, gate_t, lg, lb, tc, sub, alpha):
    T = gate_t.shape[0]
    D = lg.shape[1]
    r = D // LANES_V7X
    const = lambda i, ps: (0, 0)
    row = lambda i, ps: (i, 0)
    return pl.pallas_call(
        functools.partial(_combine_kernel, tc=tc, sub=sub, T=T, alpha=alpha),
        grid_spec=pltpu.PrefetchScalarGridSpec(
            num_scalar_prefetch=1,
            grid=(T // tc,),
            in_specs=[
                pl.BlockSpec(memory_space=pl.ANY),
                pl.BlockSpec(memory_space=pl.ANY),
                pl.BlockSpec(memory_space=pl.ANY),
                pl.BlockSpec((tc * r, LANES_V7X), row),
                pl.BlockSpec((tc, TOP_K), row),
                pl.BlockSpec((1, D), const),
                pl.BlockSpec((1, D), const),
            ],
            out_specs=[pl.BlockSpec((tc, D), row), pl.BlockSpec((tc, D), row)],
            scratch_shapes=[
                pltpu.SMEM((TOP_K * tc,), jnp.int32),
                pltpu.SMEM((tc // sub, SUBLANES_V7X, LANES_V7X), jnp.int32),
                pltpu.VMEM((2, TOP_K * sub * r, LANES_V7X), F32),
                pltpu.VMEM((TOP_K, sub * r, LANES_V7X), F32),
                pltpu.SemaphoreType.DMA,
                pltpu.SemaphoreType.DMA((2,)),
            ],
        ),
        out_shape=[jax.ShapeDtypeStruct((T, D), F32), jax.ShapeDtypeStruct((T, D), BF16)],
        compiler_params=_cparams(("arbitrary",)),
        name="moe_combine",
    )(pstart, ploc_flat, tmeta, ys, x1t, gate_t, lg, lb)


def _tile(n, pref):
    t = min(n, pref)
    assert n % t == 0, (n, pref)
    return t


def kernel(x, w_in, b_in, conv_dw_w, conv_dw_b, conv_norm_g, conv_norm_b, conv_out_w, conv_out_b, qk_conv_w, qk_conv_b, mlstm_norm_g, mlstm_out_w, w_out, ln1_g, ln1_b, router_w, router_b, moe_w1, moe_b1, moe_w2, moe_b2, ln2_g, ln2_b):
    B, S, D = x.shape
    depth, _, n_in = w_in.shape
    C = conv_dw_w.shape[2]
    W = mlstm_out_w.shape[1]
    E = router_w.shape[2]
    heads = (n_in - 2 * C - 4 * W - 2 * D) // 2
    assert C == D and W == D and n_in == 2 * C + 4 * W + 2 * heads + 2 * D
    T = B * S
    alpha = float((2 * depth) ** 0.25)
    off_i = 2 * C + 4 * W

    tm_proj = _tile(T, 1024)
    ts_conv = _tile(S, 256)
    l_chunk = _tile(S, 256)
    td = _tile(T, 1024)
    sub = _tile(td, 256)
    tm_merge = _tile(T, 2 * sub)
    blk = _tile(T * TOP_K, 512)
    n_blocks = T * TOP_K // blk + E
    n_rows = n_blocks * blk

    tn_proj = _tile(2 * D, 1024)
    assert off_i % tn_proj == 0 and 2 * heads < LANES_V7X
    b_main = jnp.concatenate([b_in[:, :off_i], b_in[:, off_i + 2 * heads:]], axis=1)[:, None, :]
    w_if = jnp.pad(w_in[:, :, off_i:off_i + 2 * heads], ((0, 0), (0, 0), (0, LANES_V7X - 2 * heads))).astype(BF16)
    b_if = jnp.pad(b_in[:, off_i:off_i + 2 * heads], ((0, 0), (0, LANES_V7X - 2 * heads)))[:, None, :]
    conv_ow = conv_out_w.astype(BF16)
    ml_ow = mlstm_out_w.astype(BF16)
    wo = w_out.astype(BF16)
    rwt = jnp.swapaxes(router_w, 1, 2).astype(BF16)
    rb = jnp.broadcast_to(router_b[:, :, None], (depth, E, LANES_V7X))

    xf = x.reshape(T, D)
    xb = xf.astype(BF16)
    for l in range(depth):
        xp = _linear(xb, w_in, l, b_main[l], off_i // tn_proj, (n_in - 2 * heads) // tn_proj, 2 * heads,
                     tm_proj, tn_proj)
        yc = _conformer(xp, conv_dw_w[l], conv_dw_b[l][None], conv_norm_g[l][None], conv_norm_b[l][None],
                        conv_ow[l], conv_out_b[l][None], B, S, ts_conv)
        ym = _mlstm(xp, xb, w_if[l], b_if[l], qk_conv_w[l][:, :W], qk_conv_b[l][None, :W],
                    qk_conv_w[l][:, W:], qk_conv_b[l][None, W:], mlstm_norm_g[l][None], ml_ow[l],
                    B, S, l_chunk, heads, 2 * C // W)
        x1t, gate, ploc, tmeta, counts = _merge(xp, yc, ym, xf, wo[l], ln1_g[l][None], ln1_b[l][None], rwt[l],
                                                rb[l], tm_merge, sub, (2 * C + 4 * W) // D, alpha)
        blocks, meta = _plan(counts, blk, n_blocks)
        ploc = ploc.reshape(TOP_K * T)
        xs = _dispatch(meta[0, :E + 1], meta[1, :E], ploc, tmeta, x1t, T, n_rows, td, sub, blk)
        ys = _experts(blocks[0, :n_blocks], meta[2, :1], xs, moe_w1, moe_b1[:, :, None, :], moe_w2,
                      moe_b2[:, :, None, :], l, blk)
        xf, xb = _combine(meta[1, :E], ploc, tmeta, ys, x1t, gate.T, ln2_g[l][None], ln2_b[l][None], td, sub, alpha)
    return xf.reshape(B, S, D)
```

```python
import functools

import jax
import jax.numpy as jnp
from jax import lax
from jax.experimental import pallas as pl
from jax.experimental.pallas import tpu as pltpu

TOP_K = 4
SWIGLU_LIMIT = 7.0
SWIGLU_ALPHA = 1.702
LN_EPS = 1e-5
LANES_V7X = 128
SUBLANES_V7X = 8
VMEM_LIMIT_V7X = 56 * 1024 * 1024

F32 = jnp.float32
BF16 = jnp.bfloat16


def _cparams(sem):
    return pltpu.CompilerParams(dimension_semantics=sem, vmem_limit_bytes=VMEM_LIMIT_V7X)


def _ln_rows(x, g, b):
    mu = jnp.mean(x, axis=-1, keepdims=True)
    xc = x - mu
    var = jnp.mean(xc * xc, axis=-1, keepdims=True)
    y = xc * lax.rsqrt(var + LN_EPS) * g
    return y if b is None else y + b


def _sigmoid(x):
    return 1.0 / (1.0 + jnp.exp(-x))


def _log_sigmoid(x):
    return jnp.minimum(x, 0.0) - jnp.log(1.0 + jnp.exp(-jnp.abs(x)))


def _rows_from_tiles(ref, n):
    r = ref.shape[0] // n
    return jnp.concatenate([ref[pl.ds(s, n, stride=r), :] for s in range(r)], axis=1)


def _rows_to_tiles(ref, val):
    n = val.shape[0]
    r = ref.shape[0] // n
    for s in range(r):
        ref[pl.ds(s, n, stride=r), :] = val[:, s * LANES_V7X:(s + 1) * LANES_V7X]


def _linear_kernel(x_ref, wlo_ref, whi_ref, b_ref, o_ref, wbf, *, n_aligned, skip, kc):
    j = pl.program_id(0)
    i = pl.program_id(1)
    K, tn = wbf.shape

    @pl.when(jnp.logical_and(i == 0, j < n_aligned))
    def _():
        wbf[...] = wlo_ref[0].astype(BF16)

    @pl.when(jnp.logical_and(i == 0, j >= n_aligned))
    def _():
        for r0 in range(0, K, kc):
            both = jnp.concatenate([wlo_ref[0, r0:r0 + kc, :], whi_ref[0, r0:r0 + kc, :]], axis=1)
            wbf[r0:r0 + kc, :] = pltpu.roll(both, 2 * tn - skip, axis=1)[:, :tn].astype(BF16)

    o_ref[...] = (jnp.dot(x_ref[...], wbf[...], preferred_element_type=F32) + b_ref[...]).astype(o_ref.dtype)


def _linear(x, w_all, layer, b, n_aligned, n_blocks, skip, tm, tn):
    T, K = x.shape
    last = -(-w_all.shape[2] // tn) - 1
    return pl.pallas_call(
        functools.partial(_linear_kernel, n_aligned=n_aligned, skip=skip, kc=min(K, 256)),
        grid=(n_blocks, T // tm),
        in_specs=[
            pl.BlockSpec((tm, K), lambda j, i: (i, 0)),
            pl.BlockSpec((1, K, tn), lambda j, i: (layer, 0, j)),
            pl.BlockSpec((1, K, tn), lambda j, i: (layer, 0, jnp.minimum(j + 1, last))),
            pl.BlockSpec((1, tn), lambda j, i: (0, j)),
        ],
        out_specs=pl.BlockSpec((tm, tn), lambda j, i: (i, j)),
        out_shape=jax.ShapeDtypeStruct((T, n_blocks * tn), BF16),
        scratch_shapes=[pltpu.VMEM((K, tn), BF16)],
        compiler_params=_cparams(("arbitrary", "arbitrary")),
        name="in_proj",
    )(x, w_all, w_all, b)


def _conformer_kernel(a_ref, g_ref, dww_ref, dwb_ref, ng_ref, nb_ref, ow_ref, ob_ref, y_ref,
                      zbuf, cbuf, *, taps, halo, rb, cb):
    s = pl.program_id(1)
    ts, C = a_ref.shape

    @pl.when(s == 0)
    def _():
        zbuf[0:halo, :] = jnp.zeros((halo, C), F32)

    @pl.when(s != 0)
    def _():
        zbuf[0:halo, :] = zbuf[ts:ts + halo, :]

    zbuf[halo:halo + ts, :] = a_ref[...].astype(F32) * _sigmoid(g_ref[...].astype(F32))
    off = halo - (taps - 1)
    sub = SUBLANES_V7X
    for r0 in range(0, ts, rb):
        for c0 in range(0, C, cb):
            acc = jnp.broadcast_to(dwb_ref[:, c0:c0 + cb], (rb, cb))
            for r in range(sub):
                wl = rb + (sub if r else 0)
                q = None
                for a in range((off + taps - 1) // sub + 1):
                    j = sub * a + r - off
                    if 0 <= j < taps:
                        assert r0 + sub * a + wl <= halo + ts
                        term = dww_ref[j:j + 1, c0:c0 + cb] * zbuf[r0 + sub * a:r0 + sub * a + wl, c0:c0 + cb]
                        q = term if q is None else q + term
                if q is not None:
                    acc = acc + q[r:r + rb, :]
            cbuf[r0:r0 + rb, c0:c0 + cb] = acc
    zn = _ln_rows(cbuf[...], ng_ref[...], nb_ref[...])
    act = zn * _sigmoid(zn)
    y_ref[...] = jnp.dot(act.astype(BF16), ow_ref[...], preferred_element_type=F32) + ob_ref[...]


def _conformer(xp, dww, dwb, ng, nb, ow, ob, B, S, ts):
    T = xp.shape[0]
    C = dww.shape[1]
    D = ow.shape[1]
    taps = dww.shape[0]
    halo = -(-(taps - 1) // SUBLANES_V7X) * SUBLANES_V7X
    nS = S // ts
    kern = functools.partial(_conformer_kernel, taps=taps, halo=halo, rb=min(64, ts), cb=min(256, C))
    const = lambda b, s: (0, 0)
    return pl.pallas_call(
        kern,
        grid=(B, nS),
        in_specs=[
            pl.BlockSpec((ts, C), lambda b, s: (b * nS + s, 0)),
            pl.BlockSpec((ts, C), lambda b, s: (b * nS + s, 1)),
            pl.BlockSpec((taps, C), const),
            pl.BlockSpec((1, C), const),
            pl.BlockSpec((1, C), const),
            pl.BlockSpec((1, C), const),
            pl.BlockSpec((C, D), const),
            pl.BlockSpec((1, D), const),
        ],
        out_specs=pl.BlockSpec((ts, D), lambda b, s: (b * nS + s, 0)),
        out_shape=jax.ShapeDtypeStruct((T, D), F32),
        scratch_shapes=[pltpu.VMEM((halo + ts, C), F32), pltpu.VMEM((ts, C), F32)],
        compiler_params=_cparams(("arbitrary", "arbitrary")),
        name="conformer",
    )(xp, xp, dww, dwb, ng, nb, ow, ob)


def _mlstm_kernel(q_ref, k_ref, v_ref, o_ref, x_ref, wif_ref, bif_ref, wq_ref, bq_ref, wk_ref, bk_ref,
                  ng_ref, ow_ref, y_ref, qbuf, kbuf, c_ref, n_ref, m_ref, hbuf, *, heads, taps, halo):
    s = pl.program_id(1)
    L, W = q_ref.shape
    dh = W // heads

    @pl.when(s == 0)
    def _():
        qbuf[0:halo, :] = jnp.zeros((halo, W), F32)
        kbuf[0:halo, :] = jnp.zeros((halo, W), F32)
        c_ref[...] = jnp.zeros(c_ref.shape, F32)
        n_ref[...] = jnp.zeros(n_ref.shape, F32)
        m_ref[...] = jnp.zeros(m_ref.shape, F32)

    @pl.when(s != 0)
    def _():
        qbuf[0:halo, :] = qbuf[L:L + halo, :]
        kbuf[0:halo, :] = kbuf[L:L + halo, :]

    qbuf[halo:halo + L, :] = q_ref[...].astype(F32)
    kbuf[halo:halo + L, :] = k_ref[...].astype(F32)
    off = halo - (taps - 1)

    def short_conv(buf, w_ref, b_ref):
        acc = b_ref[...] + w_ref[0:1, :] * buf[off:off + L, :]
        for j in range(1, taps):
            acc = acc + w_ref[j:j + 1, :] * buf[off + j:off + j + L, :]
        return acc * _sigmoid(acc)

    qc = short_conv(qbuf, wq_ref, bq_ref)
    kc = short_conv(kbuf, wk_ref, bk_ref) * (dh ** -0.5)

    gates = jnp.dot(x_ref[...], wif_ref[...], preferred_element_type=F32) + bif_ref[...]
    logf = _log_sigmoid(gates)
    row = lax.broadcasted_iota(jnp.int32, (L, L), 0)
    col = lax.broadcasted_iota(jnp.int32, (L, L), 1)
    causal = row >= col
    bcum = jnp.dot(causal.astype(F32), logf, preferred_element_type=F32, precision=lax.Precision.HIGHEST)
    gates_t = gates.T
    bcum_t = bcum.T

    for h in range(heads):
        sl = slice(h * dh, (h + 1) * dh)
        qf = qc[:, sl]
        kf = kc[:, sl]
        qh = qf.astype(BF16)
        kh = kf.astype(BF16)
        vb = v_ref[:, sl]
        vf = vb.astype(F32)
        ig_c = gates[:, h:h + 1]
        ig_r = gates_t[h:h + 1, :]
        b_c = bcum[:, heads + h:heads + h + 1]
        b_r = bcum_t[heads + h:heads + h + 1, :]
        m_prev = m_ref[h:h + 1, 0:1]
        dmat = jnp.where(causal, b_c - b_r + ig_r, -jnp.inf)
        inter = b_c + m_prev
        m_row = jnp.maximum(inter, jnp.max(dmat, axis=1, keepdims=True))
        w_intra = jnp.exp(dmat - m_row)
        w_inter = jnp.exp(inter - m_row)
        sc = lax.dot_general(qh, kh, (((1,), (1,)), ((), ())), preferred_element_type=F32) * w_intra
        cmat = c_ref[h]
        num = w_inter * jnp.dot(qh, cmat.astype(BF16), preferred_element_type=F32) + jnp.dot(
            sc.astype(BF16), vb, preferred_element_type=F32)
        nvec = n_ref[h:h + 1, :]
        den = w_inter * jnp.sum(qf * nvec, axis=1, keepdims=True) + jnp.sum(sc, axis=1, keepdims=True)
        hh = num / jnp.maximum(jnp.abs(den), jnp.exp(-m_row))
        b_last = b_c[L - 1:L, :]
        lws_c = b_last - b_c + ig_c
        lws_r = b_last - b_r + ig_r
        m_new = jnp.maximum(b_last + m_prev, jnp.max(lws_r, axis=1, keepdims=True))
        ws_c = jnp.exp(lws_c - m_new)
        decay = jnp.exp(b_last + m_prev - m_new)
        kv = lax.dot_general(kh, (vf * ws_c).astype(BF16), (((0,), (0,)), ((), ())), preferred_element_type=F32)
        c_ref[h] = decay * cmat + kv
        n_ref[h:h + 1, :] = decay * nvec + jnp.sum(kf * ws_c, axis=0, keepdims=True)
        m_ref[h:h + 1, :] = jnp.broadcast_to(m_new, (1, m_ref.shape[1]))
        hn = _ln_rows(hh, ng_ref[:, sl], None)
        hbuf[:, sl] = hn * _sigmoid(o_ref[:, sl].astype(F32))
    y_ref[...] = jnp.dot(hbuf[...].astype(BF16), ow_ref[...], preferred_element_type=F32)


def _mlstm(xp, xb, wif, bif, wq, bq, wk, bk, ng, ow, B, S, L, heads, col_q):
    T = xp.shape[0]
    D = xb.shape[1]
    W = wq.shape[1]
    taps = wq.shape[0]
    dh = W // heads
    halo = SUBLANES_V7X
    nS = S // L
    kern = functools.partial(_mlstm_kernel, heads=heads, taps=taps, halo=halo)
    const = lambda b, s: (0, 0)

    def colspec(c):
        return pl.BlockSpec((L, W), lambda b, s: (b * nS + s, c))

    return pl.pallas_call(
        kern,
        grid=(B, nS),
        in_specs=[
            colspec(col_q), colspec(col_q + 1), colspec(col_q + 2), colspec(col_q + 3),
            pl.BlockSpec((L, D), lambda b, s: (b * nS + s, 0)),
            pl.BlockSpec(wif.shape, const),
            pl.BlockSpec(bif.shape, const),
            pl.BlockSpec((taps, W), const),
            pl.BlockSpec((1, W), const),
            pl.BlockSpec((taps, W), const),
            pl.BlockSpec((1, W), const),
            pl.BlockSpec((1, W), const),
            pl.BlockSpec((W, D), const),
        ],
        out_specs=pl.BlockSpec((L, D), lambda b, s: (b * nS + s, 0)),
        out_shape=jax.ShapeDtypeStruct((T, D), F32),
        scratch_shapes=[
            pltpu.VMEM((halo + L, W), F32),
            pltpu.VMEM((halo + L, W), F32),
            pltpu.VMEM((heads, dh, dh), F32),
            pltpu.VMEM((heads, dh), F32),
            pltpu.VMEM((heads, LANES_V7X), F32),
            pltpu.VMEM((L, W), F32),
        ],
        compiler_params=_cparams(("arbitrary", "arbitrary")),
        name="mlstm",
    )(xp, xp, xp, xp, xb, wif, bif, wq, bq, wk, bk, ng, ow)


def _merge_kernel(ga_ref, gb_ref, yc_ref, ym_ref, x_ref, wo_ref, lg_ref, lb_ref, rwt_ref, rb_ref,
                  x1_ref, gate_ref, ploc_ref, tmeta_ref, cnt_ref, carry, *, alpha, sub):
    i = pl.program_id(0)
    tm = x_ref.shape[0]
    E = rwt_ref.shape[0]
    rpt = x1_ref.shape[0] // tm

    @pl.when(i == 0)
    def _():
        carry[...] = jnp.zeros(carry.shape, F32)

    mix = _sigmoid(ga_ref[...].astype(F32)) * yc_ref[...] + _sigmoid(gb_ref[...].astype(F32)) * ym_ref[...]
    r = alpha * x_ref[...] + jnp.dot(mix.astype(BF16), wo_ref[...], preferred_element_type=F32)
    x1 = _ln_rows(r, lg_ref[...], lb_ref[...])
    _rows_to_tiles(x1_ref, x1)
    logits = lax.dot_general(rwt_ref[...], x1.astype(BF16), (((1,), (1,)), ((), ())),
                             preferred_element_type=F32) + rb_ref[:, 0:1]
    eidx = lax.broadcasted_iota(jnp.int32, (E, tm), 0).astype(F32)
    work = logits
    tops, hots = [], []
    for k in range(TOP_K):
        mk = jnp.max(work, axis=0, keepdims=True)
        ik = jnp.min(jnp.where(work == mk, eidx, float(E)), axis=0, keepdims=True)
        hot = eidx == ik
        work = jnp.where(hot, -jnp.inf, work)
        tops.append(mk)
        hots.append(hot)
    exps = [jnp.exp(t - tops[0]) for t in tops]
    denom = exps[0]
    for e in exps[1:]:
        denom = denom + e
    for k in range(TOP_K):
        gate_ref[k:k + 1, :] = exps[k] / denom
    trow = lax.broadcasted_iota(jnp.int32, (sub, sub), 0)
    tcol = lax.broadcasted_iota(jnp.int32, (sub, sub), 1)
    before = (trow < tcol).astype(BF16)
    er = lax.broadcasted_iota(jnp.int32, (E, E), 0)
    ec = lax.broadcasted_iota(jnp.int32, (E, E), 1)
    tmeta_ref[...] = jnp.zeros(tmeta_ref.shape, jnp.int32)
    seen = carry[...]
    for u in range(tm // sub):
        lanes = slice(u * sub, (u + 1) * sub)
        tile_hots = [hot[:, lanes] for hot in hots]
        sel = tile_hots[0].astype(F32)
        for hot in tile_hots[1:]:
            sel = sel + hot.astype(F32)
        earlier = jnp.dot(sel.astype(BF16), before, preferred_element_type=F32)
        cnt_tile = jnp.sum(sel, axis=1, keepdims=True)
        cnt_row = jnp.sum(jnp.where(er == ec, cnt_tile, 0.0), axis=0, keepdims=True)
        pos = jnp.sum(jnp.where(ec < er, cnt_row, 0.0), axis=1, keepdims=True) + earlier
        for k in range(TOP_K):
            ploc_ref[k:k + 1, lanes] = (
                jnp.sum(jnp.where(tile_hots[k], pos, 0.0), axis=0, keepdims=True) * rpt).astype(jnp.int32)
        seen_row = jnp.sum(jnp.where(er == ec, seen[:, 0:1], 0.0), axis=0, keepdims=True)
        tmeta_ref[u, 0:1, 0:E] = cnt_row.astype(jnp.int32)
        tmeta_ref[u, 1:2, 0:E] = seen_row.astype(jnp.int32)
        seen = seen + cnt_tile
    carry[...] = seen
    cnt_ref[...] = seen


def _merge(xp, yc, ym, x, wo, lg, lb, rwt, rb, tm, sub, col_ga, alpha):
    T, D = x.shape
    E = rwt.shape[0]
    kern = functools.partial(_merge_kernel, alpha=alpha, sub=sub)
    const = lambda i: (0, 0)
    row = lambda i: (i, 0)
    krow = pl.BlockSpec((TOP_K, tm), lambda i: (0, i))
    return pl.pallas_call(
        kern,
        grid=(T // tm,),
        in_specs=[
            pl.BlockSpec((tm, D), lambda i: (i, col_ga)),
            pl.BlockSpec((tm, D), lambda i: (i, col_ga + 1)),
            pl.BlockSpec((tm, D), row),
            pl.BlockSpec((tm, D), row),
            pl.BlockSpec((tm, D), row),
            pl.BlockSpec((D, D), const),
            pl.BlockSpec((1, D), const),
            pl.BlockSpec((1, D), const),
            pl.BlockSpec((E, D), const),
            pl.BlockSpec((E, LANES_V7X), const),
        ],
        out_specs=[
            pl.BlockSpec((tm * D // LANES_V7X, LANES_V7X), row),
            krow, krow,
            pl.BlockSpec((tm // sub, SUBLANES_V7X, LANES_V7X), lambda i: (i, 0, 0)),
            pl.BlockSpec((E, LANES_V7X), const),
        ],
        out_shape=[
            jax.ShapeDtypeStruct((T * D // LANES_V7X, LANES_V7X), F32),
            jax.ShapeDtypeStruct((TOP_K, T), F32),
            jax.ShapeDtypeStruct((TOP_K, T), jnp.int32),
            jax.ShapeDtypeStruct((T // sub, SUBLANES_V7X, LANES_V7X), jnp.int32),
            jax.ShapeDtypeStruct((E, LANES_V7X), F32),
        ],
        scratch_shapes=[pltpu.VMEM((E, LANES_V7X), F32)],
        compiler_params=_cparams(("arbitrary",)),
        name="merge_router",
    )(xp, xp, yc, ym, x, wo, lg, lb, rwt, rb)


def _plan_kernel(cnt_ref, blk_ref, meta_ref, *, blk):
    E = cnt_ref.shape[0]
    nbp = blk_ref.shape[1]
    cnt = cnt_ref[:, 0:1]
    padded = jnp.floor((cnt + (blk - 1)) / blk) * blk
    er = lax.broadcasted_iota(jnp.int32, (E, E), 0)
    ec = lax.broadcasted_iota(jnp.int32, (E, E), 1)
    padded_row = jnp.sum(jnp.where(er == ec, padded, 0.0), axis=0, keepdims=True)
    pstart = jnp.sum(jnp.where(ec < er, padded_row, 0.0), axis=1, keepdims=True)
    pend = pstart + padded
    pend_row = jnp.sum(jnp.where(er <= ec, padded, 0.0), axis=0, keepdims=True)
    n_used = jnp.sum(padded_row, axis=1, keepdims=True) / blk
    meta_ref[...] = jnp.broadcast_to(n_used, meta_ref.shape).astype(jnp.int32)
    meta_ref[0:1, 0:E] = pend_row.astype(jnp.int32)
    meta_ref[1:2, 0:E] = (pend_row - padded_row).astype(jnp.int32)
    first_row = lax.broadcasted_iota(jnp.int32, (E, nbp), 1).astype(F32) * blk
    inside = jnp.logical_and(pstart <= first_row, first_row < pend)
    eid = lax.broadcasted_iota(jnp.int32, (E, nbp), 0).astype(F32)
    block_e = jnp.sum(jnp.where(inside, eid, 0.0), axis=0, keepdims=True)
    valid = jnp.sum(jnp.where(inside, 1.0, 0.0), axis=0, keepdims=True)
    e_last = jnp.max(jnp.where(padded > 0.0, eid[:, 0:1], 0.0), axis=0, keepdims=True)
    block_e = jnp.where(valid > 0.0, block_e, e_last)
    blk_ref[...] = jnp.broadcast_to(block_e.astype(jnp.int32), blk_ref.shape)


def _plan(counts, blk, n_blocks):
    E = counts.shape[0]
    nbp = -(-n_blocks // LANES_V7X) * LANES_V7X
    return pl.pallas_call(
        functools.partial(_plan_kernel, blk=blk),
        out_shape=[
            jax.ShapeDtypeStruct((SUBLANES_V7X, nbp), jnp.int32),
            jax.ShapeDtypeStruct((SUBLANES_V7X, LANES_V7X), jnp.int32),
        ],
        name="route_plan",
    )(counts)


def _run_copies(n, local_row, global_row, buf, hbm, sem, rpt, max_rows, to_hbm):
    for b in range(max_rows.bit_length()):
        size = 1 << b
        done = n & ~(2 * size - 1)

        @pl.when((n & size) != 0)
        def _():
            loc = buf.at[pl.ds(pl.multiple_of((local_row + done) * rpt, rpt), size * rpt)]
            glob = hbm.at[pl.ds(pl.multiple_of((global_row + done) * rpt, rpt), size * rpt)]
            src, dst = (loc, glob) if to_hbm else (glob, loc)
            pltpu.make_async_copy(src, dst, sem).start()


def _tile_runs(j, pstart_ref, msm, buf, hbm, sem, rpt, sub, to_hbm):
    def runs(e, off):
        n = msm[j, 0, e]
        _run_copies(n, off, pstart_ref[e] + msm[j, 1, e], buf, hbm, sem, rpt, sub, to_hbm)
        return off + n

    lax.fori_loop(0, pstart_ref.shape[0], runs, 0)


def _step_indices(ploc_hbm, tmeta_hbm, psm, msm, isem, td, n_sub, T):
    i = pl.program_id(0)
    slot = i % 2

    def copies(step, s):
        cs = [pltpu.make_async_copy(ploc_hbm.at[pl.ds(k * T + step * td, td)],
                                    psm.at[pl.ds(pl.multiple_of((s * TOP_K + k) * td, td), td)], isem.at[s])
              for k in range(TOP_K)]
        cs.append(pltpu.make_async_copy(tmeta_hbm.at[pl.ds(step * n_sub, n_sub)], msm.at[s], isem.at[s]))
        return cs

    @pl.when(i == 0)
    def _():
        for c in copies(0, 0):
            c.start()

    for c in copies(i, slot):
        c.wait()

    @pl.when(i + 1 < pl.num_programs(0))
    def _():
        for c in copies(i + 1, 1 - slot):
            c.start()

    return slot * (TOP_K * td), msm.at[slot]


def _dispatch_kernel(pend_ref, pstart_ref, ploc_hbm, tmeta_hbm, x_ref, xs_hbm, psm, msm_all, sbuf, zbuf, isem, rsem,
                     zsem, *, td, sub, T, blk, n_blocks):
    i = pl.program_id(0)
    E = pend_ref.shape[0] - 1
    r = x_ref.shape[0] // td
    n_sub = td // sub
    pbase, msm = _step_indices(ploc_hbm, tmeta_hbm, psm, msm_all, isem, td, n_sub, T)

    @pl.when(i == 0)
    def _():
        zbuf[...] = jnp.zeros(zbuf.shape, zbuf.dtype)
        n_used = pend_ref[E]

        def clear(start):
            return pltpu.make_async_copy(zbuf, xs_hbm.at[pl.ds(pl.multiple_of(start * r, blk * r), blk * r)], zsem)

        def clears(fn):
            for e in range(E):
                @pl.when(pend_ref[e] >= blk)
                def _():
                    fn(clear(pend_ref[e] - blk))

                @pl.when(n_used + e < n_blocks)
                def _():
                    fn(clear((n_used + e) * blk))

        clears(lambda c: c.start())
        clears(lambda c: c.wait())

    def drain(slot):
        pltpu.make_async_copy(sbuf.at[slot], xs_hbm.at[pl.ds(0, TOP_K * sub * r)], rsem.at[slot]).wait()

    for j in range(n_sub):
        slot = j % 2

        def place(t, c):
            row = x_ref[pl.ds(pl.multiple_of((j * sub + t) * r, r), r), :]
            for k in range(TOP_K):
                sbuf[slot, pl.ds(pl.multiple_of(psm[pbase + (k * td + j * sub) + t], r), r), :] = row
            return c

        lax.fori_loop(0, sub, place, 0, unroll=8)
        _tile_runs(j, pstart_ref, msm, sbuf.at[slot], xs_hbm, rsem.at[slot], r, sub, True)
        if j >= 1:
            drain(1 - slot)
    drain((n_sub - 1) % 2)


def _dispatch(pend, pstart, ploc_flat, tmeta, x1t, T, n_rows, td, sub, blk):
    r = x1t.shape[0] // T
    return pl.pallas_call(
        functools.partial(_dispatch_kernel, td=td, sub=sub, T=T, blk=blk, n_blocks=n_rows // blk),
        grid_spec=pltpu.PrefetchScalarGridSpec(
            num_scalar_prefetch=2,
            grid=(T // td,),
            in_specs=[
                pl.BlockSpec(memory_space=pl.ANY),
                pl.BlockSpec(memory_space=pl.ANY),
                pl.BlockSpec((td * r, LANES_V7X), lambda i, pend, pstart: (i, 0)),
            ],
            out_specs=pl.BlockSpec(memory_space=pl.ANY),
            scratch_shapes=[
                pltpu.SMEM((2 * TOP_K * td,), jnp.int32),
                pltpu.SMEM((2, td // sub, SUBLANES_V7X, LANES_V7X), jnp.int32),
                pltpu.VMEM((2, TOP_K * sub * r, LANES_V7X), F32),
                pltpu.VMEM((blk * r, LANES_V7X), F32),
                pltpu.SemaphoreType.DMA((2,)),
                pltpu.SemaphoreType.DMA((2,)),
                pltpu.SemaphoreType.DMA,
            ],
        ),
        out_shape=jax.ShapeDtypeStruct((n_rows * r, LANES_V7X), F32),
        compiler_params=_cparams(("arbitrary",)),
        name="moe_dispatch",
    )(pend, pstart, ploc_flat, tmeta, x1t)


def _expert_kernel(be_ref, nu_ref, xs_ref, w1_ref, b1_ref, w2_ref, b2_ref, y_ref, w1b, w2b):
    i = pl.program_id(0)
    D, F2 = w1b.shape
    F = F2 // 2

    @pl.when(jnp.logical_or(i == 0, be_ref[i] != be_ref[jnp.maximum(i - 1, 0)]))
    def _():
        w1b[...] = w1_ref[0, 0].astype(BF16)
        w2b[...] = w2_ref[0, 0].astype(BF16)

    @pl.when(i < nu_ref[0])
    def _():
        xs = _rows_from_tiles(xs_ref, xs_ref.shape[0] * LANES_V7X // D)
        h = jnp.dot(xs.astype(BF16), w1b[...], preferred_element_type=F32) + b1_ref[0, 0]
        g = jnp.minimum(h[:, :F], SWIGLU_LIMIT)
        lin = jnp.clip(h[:, F:], -SWIGLU_LIMIT, SWIGLU_LIMIT)
        act = g * _sigmoid(SWIGLU_ALPHA * g) * (lin + 1.0)
        _rows_to_tiles(y_ref, jnp.dot(act.astype(BF16), w2b[...], preferred_element_type=F32) + b2_ref[0, 0])

    @pl.when(i >= nu_ref[0])
    def _():
        y_ref[...] = jnp.zeros(y_ref.shape, F32)


def _experts(block_e, n_used, xs, w1, b1, w2, b2, layer, blk):
    _, E, D, F2 = w1.shape
    F = w2.shape[2]
    r = D // LANES_V7X
    NR = xs.shape[0] // r
    wmap = lambda i, be, nu: (layer, be[i], 0, 0)
    return pl.pallas_call(
        _expert_kernel,
        grid_spec=pltpu.PrefetchScalarGridSpec(
            num_scalar_prefetch=2,
            grid=(NR // blk,),
            in_specs=[
                pl.BlockSpec((blk * r, LANES_V7X), lambda i, be, nu: (jnp.minimum(i, nu[0] - 1), 0)),
                pl.BlockSpec((1, 1, D, F2), wmap),
                pl.BlockSpec((1, 1, 1, F2), wmap),
                pl.BlockSpec((1, 1, F, D), wmap),
                pl.BlockSpec((1, 1, 1, D), wmap),
            ],
            out_specs=pl.BlockSpec((blk * r, LANES_V7X), lambda i, be, nu: (i, 0)),
            scratch_shapes=[pltpu.VMEM((D, F2), BF16), pltpu.VMEM((F, D), BF16)],
        ),
        out_shape=jax.ShapeDtypeStruct((NR * r, LANES_V7X), F32),
        compiler_params=_cparams(("arbitrary",)),
        name="moe_experts",
    )(block_e, n_used, xs, w1, b1, w2, b2)


def _combine_kernel(pstart_ref, ploc_hbm, tmeta_hbm, ys_hbm, x1_ref, gate_ref, lg_ref, lb_ref, o_ref, ob_ref,
                    psm, msm_all, sbuf, gbuf, isem, rsem, *, tc, sub, T, alpha):
    r = x1_ref.shape[0] // tc
    n_sub = tc // sub
    pbase, msm = _step_indices(ploc_hbm, tmeta_hbm, psm, msm_all, isem, tc, n_sub, T)

    def fetch(j):
        _tile_runs(j, pstart_ref, msm, sbuf.at[j % 2], ys_hbm, rsem.at[j % 2], r, sub, False)

    fetch(0)
    for j in range(n_sub):
        slot = j % 2
        if j + 1 < n_sub:
            fetch(j + 1)
        pltpu.make_async_copy(ys_hbm.at[pl.ds(0, TOP_K * sub * r)], sbuf.at[slot], rsem.at[slot]).wait()

        def pick(t, c):
            for k in range(TOP_K):
                gbuf[k, pl.ds(pl.multiple_of(t * r, r), r), :] = sbuf[
                    slot, pl.ds(pl.multiple_of(psm[pbase + (k * tc + j * sub) + t], r), r), :]
            return c

        lax.fori_loop(0, sub, pick, 0, unroll=8)
        rows = slice(j * sub, (j + 1) * sub)
        gate = gate_ref[rows, :]
        y = gate[:, 0:1] * _rows_from_tiles(gbuf.at[0], sub)
        for k in range(1, TOP_K):
            y = y + gate[:, k:k + 1] * _rows_from_tiles(gbuf.at[k], sub)
        x1 = _rows_from_tiles(x1_ref.at[pl.ds(j * sub * r, sub * r)], sub)
        out = _ln_rows(alpha * x1 + y, lg_ref[...], lb_ref[...])
        o_ref[rows, :] = out
        ob_ref[rows, :] = out.astype(BF16)


def _combine(pstart, ploc_flat, tmeta, ys, x1t, gate_t, lg, lb, tc, sub, alpha):
    T = gate_t.shape[0]
    D = lg.shape[1]
    r = D // LANES_V7X
    const = lambda i, ps: (0, 0)
    row = lambda i, ps: (i, 0)
    return pl.pallas_call(
        functools.partial(_combine_kernel, tc=tc, sub=sub, T=T, alpha=alpha),
        grid_spec=pltpu.PrefetchScalarGridSpec(
            num_scalar_prefetch=1,
            grid=(T // tc,),
            in_specs=[
                pl.BlockSpec(memory_space=pl.ANY),
                pl.BlockSpec(memory_space=pl.ANY),
                pl.BlockSpec(memory_space=pl.ANY),
                pl.BlockSpec((tc * r, LANES_V7X), row),
                pl.BlockSpec((tc, TOP_K), row),
                pl.BlockSpec((1, D), const),
                pl.BlockSpec((1, D), const),
            ],
            out_specs=[pl.BlockSpec((tc, D), row), pl.BlockSpec((tc, D), row)],
            scratch_shapes=[
                pltpu.SMEM((2 * TOP_K * tc,), jnp.int32),
                pltpu.SMEM((2, tc // sub, SUBLANES_V7X, LANES_V7X), jnp.int32),
                pltpu.VMEM((2, TOP_K * sub * r, LANES_V7X), F32),
                pltpu.VMEM((TOP_K, sub * r, LANES_V7X), F32),
                pltpu.SemaphoreType.DMA((2,)),
                pltpu.SemaphoreType.DMA((2,)),
            ],
        ),
        out_shape=[jax.ShapeDtypeStruct((T, D), F32), jax.ShapeDtypeStruct((T, D), BF16)],
        compiler_params=_cparams(("arbitrary",)),
        name="moe_combine",
    )(pstart, ploc_flat, tmeta, ys, x1t, gate_t, lg, lb)


def _tile(n, pref):
    t = min(n, pref)
    assert n % t == 0, (n, pref)
    return t


def kernel(x, w_in, b_in, conv_dw_w, conv_dw_b, conv_norm_g, conv_norm_b, conv_out_w, conv_out_b, qk_conv_w, qk_conv_b, mlstm_norm_g, mlstm_out_w, w_out, ln1_g, ln1_b, router_w, router_b, moe_w1, moe_b1, moe_w2, moe_b2, ln2_g, ln2_b):
    B, S, D = x.shape
    depth, _, n_in = w_in.shape
    C = conv_dw_w.shape[2]
    W = mlstm_out_w.shape[1]
    E = router_w.shape[2]
    heads = (n_in - 2 * C - 4 * W - 2 * D) // 2
    assert C == D and W == D and n_in == 2 * C + 4 * W + 2 * heads + 2 * D
    T = B * S
    alpha = float((2 * depth) ** 0.25)
    off_i = 2 * C + 4 * W

    tm_proj = _tile(T, 1024)
    ts_conv = _tile(S, 256)
    l_chunk = _tile(S, 256)
    td = _tile(T, 1024)
    sub = _tile(td, 256)
    tm_merge = _tile(T, 2 * sub)
    blk = _tile(T * TOP_K, 512)
    n_blocks = T * TOP_K // blk + E
    n_rows = n_blocks * blk

    tn_proj = _tile(2 * D, 1024)
    assert off_i % tn_proj == 0 and 2 * heads < LANES_V7X
    b_main = jnp.concatenate([b_in[:, :off_i], b_in[:, off_i + 2 * heads:]], axis=1)[:, None, :]
    w_if = jnp.pad(w_in[:, :, off_i:off_i + 2 * heads], ((0, 0), (0, 0), (0, LANES_V7X - 2 * heads))).astype(BF16)
    b_if = jnp.pad(b_in[:, off_i:off_i + 2 * heads], ((0, 0), (0, LANES_V7X - 2 * heads)))[:, None, :]
    conv_ow = conv_out_w.astype(BF16)
    ml_ow = mlstm_out_w.astype(BF16)
    wo = w_out.astype(BF16)
    rwt = jnp.swapaxes(router_w, 1, 2).astype(BF16)
    rb = jnp.broadcast_to(router_b[:, :, None], (depth, E, LANES_V7X))

    xf = x.reshape(T, D)
    xb = xf.astype(BF16)
    for l in range(depth):
        xp = _linear(xb, w_in, l, b_main[l], off_i // tn_proj, (n_in - 2 * heads) // tn_proj, 2 * heads,
                     tm_proj, tn_proj)
        yc = _conformer(xp, conv_dw_w[l], conv_dw_b[l][None], conv_norm_g[l][None], conv_norm_b[l][None],
                        conv_ow[l], conv_out_b[l][None], B, S, ts_conv)
        ym = _mlstm(xp, xb, w_if[l], b_if[l], qk_conv_w[l][:, :W], qk_conv_b[l][None, :W],
                    qk_conv_w[l][:, W:], qk_conv_b[l][None, W:], mlstm_norm_g[l][None], ml_ow[l],
                    B, S, l_chunk, heads, 2 * C // W)
        x1t, gate, ploc, tmeta, counts = _merge(xp, yc, ym, xf, wo[l], ln1_g[l][None], ln1_b[l][None], rwt[l],
                                                rb[l], tm_merge, sub, (2 * C + 4 * W) // D, alpha)
        blocks, meta = _plan(counts, blk, n_blocks)
        ploc = ploc.reshape(TOP_K * T)
        xs = _dispatch(meta[0, :E + 1], meta[1, :E], ploc, tmeta, x1t, T, n_rows, td, sub, blk)
        ys = _experts(blocks[0, :n_blocks], meta[2, :1], xs, moe_w1, moe_b1[:, :, None, :], moe_w2,
                      moe_b2[:, :, None, :], l, blk)
        xf, xb = _combine(meta[1, :E], ploc, tmeta, ys, x1t, gate.T, ln2_g[l][None], ln2_b[l][None], td, sub, alpha)
    return xf.reshape(B, S, D)
```
